```python
import math
import jax, jax.numpy as jnp
from jax import lax
import numpy as np

D_MODEL = 4096
BATCH = 4
SEQ = 2048
DEPTH = 2
DEC_BATCH = 32
DEC_SEQ = 1
PAST_LEN = 16384
PAGE_SIZE = 128

N_MIXERS = 2
WINDOW = 128
ATTN_HEAD_DIM = 64
N_Q_HEADS = D_MODEL // ATTN_HEAD_DIM
N_KV_HEADS = N_Q_HEADS // 8
Q_PER_KV = N_Q_HEADS // N_KV_HEADS
Q_DIM = N_Q_HEADS * ATTN_HEAD_DIM
KV_DIM = N_KV_HEADS * ATTN_HEAD_DIM
QKV_DIM = Q_DIM + 2 * KV_DIM
HGRN_EXPAND = 128
HGRN_HEADS = D_MODEL // HGRN_EXPAND
HGRN_KEY_DIM = HGRN_EXPAND
HGRN_VAL_DIM = D_MODEL // HGRN_HEADS
HGRN_CHUNK = 64
D_FF = -(-8 * D_MODEL // (3 * 256)) * 256
N_ATTN_LAYERS = (DEPTH + 1) // 2
N_HGRN_LAYERS = DEPTH // 2
RMS_EPS = 1e-5

kernel_name = "hybrid_swa_sink_hgrn2_decode_step"


def rmsnorm(x, w):
    xf = x.astype(jnp.float32)
    y = xf * lax.rsqrt(jnp.mean(xf * xf, axis=-1, keepdims=True) + RMS_EPS)
    return (y * w.astype(jnp.float32)).astype(x.dtype)


def sink_softmax_weights(s, sink):
    m = jnp.maximum(jnp.max(s, axis=-1, keepdims=True), sink)
    e = jnp.exp(s - m)
    den = jnp.sum(e, axis=-1, keepdims=True) + jnp.exp(sink - m)
    return e / den


def split_qkv(xn, w_qkv, b_qkv):
    B, L, _ = xn.shape
    qkv = xn @ w_qkv + b_qkv
    q = qkv[..., :Q_DIM].reshape(B, L, N_KV_HEADS, Q_PER_KV, ATTN_HEAD_DIM)
    k = qkv[..., Q_DIM:Q_DIM + KV_DIM].reshape(B, L, N_KV_HEADS, ATTN_HEAD_DIM)
    v = qkv[..., Q_DIM + KV_DIM:].reshape(B, L, N_KV_HEADS, ATTN_HEAD_DIM)
    return q, k, v


def swa_prompt(xn, w_qkv, b_qkv, sinks, w_o, b_o):
    B, S, _ = xn.shape
    NB = S // WINDOW
    q, k, v = split_qkv(xn, w_qkv, b_qkv)
    qb = q.reshape(B, NB, WINDOW, N_KV_HEADS, Q_PER_KV, ATTN_HEAD_DIM)
    kb = k.reshape(B, NB, WINDOW, N_KV_HEADS, ATTN_HEAD_DIM)
    vb = v.reshape(B, NB, WINDOW, N_KV_HEADS, ATTN_HEAD_DIM)
    zk = jnp.zeros_like(kb[:, :1])
    kk = jnp.concatenate([jnp.concatenate([zk, kb[:, :-1]], axis=1), kb], axis=2)
    vv = jnp.concatenate([jnp.concatenate([zk, vb[:, :-1]], axis=1), vb], axis=2)
    scale = ATTN_HEAD_DIM ** -0.5
    s = jnp.einsum('bnqhgd,bnkhd->bnhgqk', qb, kk, preferred_element_type=jnp.float32) * scale
    qi = jnp.arange(WINDOW)[:, None]
    kj = jnp.arange(2 * WINDOW)[None, :]
    diff = qi + WINDOW - kj
    band = (diff >= 0) & (diff < WINDOW)
    valid_key = (jnp.arange(NB)[:, None, None] > 0) | (kj >= WINDOW)[None]
    mask = band[None] & valid_key
    s = jnp.where(mask[None, :, None, None], s, -jnp.inf)
    sink = sinks.astype(jnp.float32).reshape(N_KV_HEADS, Q_PER_KV, 1, 1)
    p = sink_softmax_weights(s, sink)
    o = jnp.einsum('bnhgqk,bnkhd->bnqhgd', p, vv.astype(jnp.float32))
    o = o.reshape(B, S, Q_DIM).astype(xn.dtype)
    return o @ w_o + b_o, k[:, -WINDOW:], v[:, -WINDOW:]


def swa_sample(xn, ck, cv, w_qkv, b_qkv, sinks, w_o, b_o):
    B, L, _ = xn.shape
    q, k, v = split_qkv(xn, w_qkv, b_qkv)
    kk = jnp.concatenate([ck.astype(k.dtype), k], axis=1)
    vv = jnp.concatenate([cv.astype(v.dtype), v], axis=1)
    scale = ATTN_HEAD_DIM ** -0.5
    s = jnp.einsum('blhgd,bkhd->bhglk', q, kk, preferred_element_type=jnp.float32) * scale
    diff = (jnp.arange(L)[:, None] + WINDOW) - jnp.arange(WINDOW + L)[None, :]
    mask = (diff >= 0) & (diff < WINDOW)
    s = jnp.where(mask, s, -jnp.inf)
    sink = sinks.astype(jnp.float32).reshape(N_KV_HEADS, Q_PER_KV, 1, 1)
    p = sink_softmax_weights(s, sink)
    o = jnp.einsum('bhglk,bkhd->blhgd', p, vv.astype(jnp.float32))
    o = o.reshape(B, L, Q_DIM).astype(xn.dtype)
    return o @ w_o + b_o, kk[:, -WINDOW:], vv[:, -WINDOW:]


def gla_chunked(q, k, v, log_f, s0):
    B, L, H, K = q.shape
    V = v.shape[-1]
    C = HGRN_CHUNK if L % HGRN_CHUNK == 0 else L
    n = L // C

    def to_chunks(a):
        return a.reshape(B, n, C, H, a.shape[-1]).transpose(1, 0, 3, 2, 4)

    tri = jnp.tril(jnp.ones((C, C), dtype=bool))[:, :, None]

    def step(S, inp):
        qc, kc, vc, lc = inp
        b = jnp.cumsum(lc, axis=2)
        o_inter = jnp.einsum('bhck,bhkv->bhcv', qc * jnp.exp(b), S)
        diff = b[:, :, :, None, :] - b[:, :, None, :, :]
        decay = jnp.exp(jnp.where(tri, diff, -jnp.inf))
        A = jnp.einsum('bhtk,bhsk,bhtsk->bhts', qc, kc, decay)
        o_intra = jnp.einsum('bhts,bhsv->bhtv', A, vc)
        b_last = b[:, :, -1:, :]
        S_new = jnp.exp(b_last[:, :, 0, :])[..., None] * S + jnp.einsum(
            'bhck,bhcv->bhkv', kc * jnp.exp(b_last - b), vc)
        return S_new, o_inter + o_intra

    S_fin, o = lax.scan(step, s0, (to_chunks(q), to_chunks(k), to_chunks(v), to_chunks(log_f)))
    o = o.transpose(1, 0, 3, 2, 4).reshape(B, L, H, V)
    return o, S_fin


def hgrn2_mix(xn, s0, w_in, lb, norm_w, w_o):
    B, L, _ = xn.shape
    q, f, i, g = jnp.split(xn @ w_in, 4, axis=-1)
    q = jax.nn.silu(q.astype(jnp.float32)) * (HGRN_KEY_DIM ** -0.5)
    forget = lb + (1.0 - lb) * jax.nn.sigmoid(f.astype(jnp.float32))
    k = 1.0 - forget
    log_f = jnp.log(forget)
    hk = lambda a: a.reshape(B, L, HGRN_HEADS, HGRN_KEY_DIM)
    o, s_fin = gla_chunked(hk(q), hk(k), i.astype(jnp.float32).reshape(B, L, HGRN_HEADS, HGRN_VAL_DIM),
                           hk(log_f), s0.astype(jnp.float32))
    o = o * lax.rsqrt(jnp.mean(o * o, axis=-1, keepdims=True) + RMS_EPS) * norm_w.astype(jnp.float32)
    o = o.reshape(B, L, D_MODEL) * jax.nn.silu(g.astype(jnp.float32))
    return o.astype(xn.dtype) @ w_o, s_fin


def swiglu(xn, w_in, w_out):
    gate, up = jnp.split(xn @ w_in, 2, axis=-1)
    return (jax.nn.silu(gate) * up) @ w_out


def setup_inputs(seed: int = 0) -> dict:
    key = jax.random.key(seed)
    ks = jax.random.split(key, 24)
    nrm = lambda k, shape, sc: jax.random.normal(k, shape, jnp.float32) * sc
    D = D_MODEL
    return {
        "x_prompt": nrm(ks[0], (BATCH, SEQ, D), 1.0),
        "x_sample": nrm(ks[1], (DEC_BATCH, DEC_SEQ, D), 1.0),
        "cache_k": nrm(ks[2], (N_ATTN_LAYERS, DEC_BATCH, WINDOW, N_KV_HEADS, ATTN_HEAD_DIM), 1.0),
        "cache_v": nrm(ks[3], (N_ATTN_LAYERS, DEC_BATCH, WINDOW, N_KV_HEADS, ATTN_HEAD_DIM), 1.0),
        "state_hgrn": nrm(ks[4], (N_HGRN_LAYERS, DEC_BATCH, HGRN_HEADS, HGRN_KEY_DIM, HGRN_VAL_DIM), 0.3),
        "norm_mix": 1.0 + nrm(ks[5], (DEPTH, D), 0.02),
        "norm_ffn": 1.0 + nrm(ks[6], (DEPTH, D), 0.02),
        "norm_final": 1.0 + nrm(ks[7], (D,), 0.02),
        "attn_w_qkv": nrm(ks[8], (N_ATTN_LAYERS, D, QKV_DIM), D ** -0.5),
        "attn_b_qkv": nrm(ks[9], (N_ATTN_LAYERS, QKV_DIM), 0.02),
        "attn_sinks": nrm(ks[10], (N_ATTN_LAYERS, N_Q_HEADS), 0.5),
        "attn_w_o": nrm(ks[11], (N_ATTN_LAYERS, Q_DIM, D), Q_DIM ** -0.5),
        "attn_b_o": nrm(ks[12], (N_ATTN_LAYERS, D), 0.02),
        "hgrn_w_in": nrm(ks[13], (N_HGRN_LAYERS, D, 4 * D), D ** -0.5),
        "hgrn_lower_bounds": nrm(ks[14], (DEPTH, D), 0.1),
        "hgrn_norm": 1.0 + nrm(ks[15], (N_HGRN_LAYERS, HGRN_VAL_DIM), 0.02),
        "hgrn_w_o": nrm(ks[16], (N_HGRN_LAYERS, D, D), D ** -0.5),
        "ffn_w_in": nrm(ks[17], (DEPTH, D, 2 * D_FF), D ** -0.5),
        "ffn_w_out": nrm(ks[18], (DEPTH, D_FF, D), D_FF ** -0.5),
    }


def reference(x_prompt, x_sample, cache_k, cache_v, state_hgrn,
              norm_mix, norm_ffn, norm_final,
              attn_w_qkv, attn_b_qkv, attn_sinks, attn_w_o, attn_b_o,
              hgrn_w_in, hgrn_lower_bounds, hgrn_norm, hgrn_w_o,
              ffn_w_in, ffn_w_out):
    p_lb = jax.nn.softmax(hgrn_lower_bounds.astype(jnp.float32), axis=0)
    lower_bounds = jnp.cumsum(p_lb, axis=0) - p_lb[0:1]

    hp, hs = x_prompt, x_sample
    wk_p, wv_p, st_p, wk_s, wv_s, st_s = [], [], [], [], [], []
    for i in range(DEPTH):
        xp_n = rmsnorm(hp, norm_mix[i])
        xs_n = rmsnorm(hs, norm_mix[i])
        if i % N_MIXERS == 0:
            a = i // N_MIXERS
            mp, kp, vp = swa_prompt(xp_n, attn_w_qkv[a], attn_b_qkv[a], attn_sinks[a], attn_w_o[a], attn_b_o[a])
            ms, ks_, vs_ = swa_sample(xs_n, cache_k[a], cache_v[a], attn_w_qkv[a], attn_b_qkv[a],
                                      attn_sinks[a], attn_w_o[a], attn_b_o[a])
            wk_p.append(kp.astype(cache_k.dtype)); wv_p.append(vp.astype(cache_v.dtype))
            wk_s.append(ks_.astype(cache_k.dtype)); wv_s.append(vs_.astype(cache_v.dtype))
        else:
            r = i // N_MIXERS
            s0_p = jnp.zeros((hp.shape[0], HGRN_HEADS, HGRN_KEY_DIM, HGRN_VAL_DIM), jnp.float32)
            mp, sp = hgrn2_mix(xp_n, s0_p, hgrn_w_in[r], lower_bounds[i], hgrn_norm[r], hgrn_w_o[r])
            ms, ss = hgrn2_mix(xs_n, state_hgrn[r], hgrn_w_in[r], lower_bounds[i], hgrn_norm[r], hgrn_w_o[r])
            st_p.append(sp.astype(state_hgrn.dtype)); st_s.append(ss.astype(state_hgrn.dtype))
        hp = hp + mp
        hs = hs + ms
        hp = hp + swiglu(rmsnorm(hp, norm_ffn[i]), ffn_w_in[i], ffn_w_out[i])
        hs = hs + swiglu(rmsnorm(hs, norm_ffn[i]), ffn_w_in[i], ffn_w_out[i])

    y_prompt = rmsnorm(hp, norm_final)
    y_sample = rmsnorm(hs, norm_final)
    win_k_prompt = jnp.stack(wk_p)
    win_v_prompt = jnp.stack(wv_p)
    state_prompt = jnp.stack(st_p)
    win_k_sample = jnp.stack(wk_s)
    win_v_sample = jnp.stack(wv_s)
    state_sample = jnp.stack(st_s)
    return (y_prompt, y_sample, win_k_prompt, win_v_prompt, state_prompt, win_k_sample, win_v_sample, state_sample)
```

```python
import functools

import jax
import jax.numpy as jnp
from jax import lax
from jax.experimental import pallas as pl
from jax.experimental.pallas import tpu as pltpu

RMS_EPS = 1e-5
BF16 = jnp.bfloat16
F32 = jnp.float32

V7X_VMEM_LIMIT_BYTES = 56 * 1024 * 1024
GLA_CHUNK = 128


def _params(*sem):
    return pltpu.CompilerParams(dimension_semantics=sem, vmem_limit_bytes=V7X_VMEM_LIMIT_BYTES)


def _tile(n, pref):
    return pref if n % pref == 0 else n


def _rmsnorm_rows(x, w):
    ms = jnp.mean(x * x, axis=-1, keepdims=True)
    return (x * lax.rsqrt(ms + RMS_EPS)) * w


def _rmsnorm_kernel(x_ref, w_ref, o_ref):
    o_ref[...] = _rmsnorm_rows(x_ref[...], w_ref[...]).astype(o_ref.dtype)


def rmsnorm(x, w, out_dtype):
    m, d = x.shape
    tm = _tile(m, 256)
    return pl.pallas_call(
        _rmsnorm_kernel,
        out_shape=jax.ShapeDtypeStruct((m, d), out_dtype),
        grid=(m // tm,),
        in_specs=[pl.BlockSpec((tm, d), lambda i: (i, 0)), pl.BlockSpec((1, d), lambda i: (0, 0))],
        out_specs=pl.BlockSpec((tm, d), lambda i: (i, 0)),
        compiler_params=_params("parallel"),
        name="rmsnorm",
    )(x, w.reshape(1, d))


def _matmul_kernel(*refs, has_bias, has_res):
    x_ref, w_ref = refs[0], refs[1]
    o_ref = refs[-1]
    acc = jnp.dot(x_ref[...], w_ref[...], preferred_element_type=F32)
    pos = 2
    if has_bias:
        acc = acc + refs[pos][...]
        pos += 1
    if has_res:
        acc = acc + refs[pos][...]
    o_ref[...] = acc.astype(o_ref.dtype)


def matmul(x, w, bias=None, res=None, out_dtype=F32):
    m, k = x.shape
    n = w.shape[1]
    tm = _tile(m, 1024)
    tn = _tile(n, 512)
    in_specs = [pl.BlockSpec((tm, k), lambda i, j: (i, 0)), pl.BlockSpec((k, tn), lambda i, j: (0, j))]
    args = [x, w]
    if bias is not None:
        in_specs.append(pl.BlockSpec((1, tn), lambda i, j: (0, j)))
        args.append(bias.reshape(1, n))
    if res is not None:
        in_specs.append(pl.BlockSpec((tm, tn), lambda i, j: (i, j)))
        args.append(res)
    return pl.pallas_call(
        functools.partial(_matmul_kernel, has_bias=bias is not None, has_res=res is not None),
        out_shape=jax.ShapeDtypeStruct((m, n), out_dtype),
        grid=(m // tm, n // tn),
        in_specs=in_specs,
        out_specs=pl.BlockSpec((tm, tn), lambda i, j: (i, j)),
        compiler_params=_params("parallel", "arbitrary"),
        name="matmul",
    )(*args)


def _ffn_kernel(h_ref, nw_ref, wg_ref, wu_ref, wo_ref, o_ref, xn_ref):
    f = pl.program_id(1)

    @pl.when(f == 0)
    def _():
        h = h_ref[...]
        xn_ref[...] = _rmsnorm_rows(h, nw_ref[...]).astype(BF16)
        o_ref[...] = h

    xn = xn_ref[...]
    g = jnp.dot(xn, wg_ref[...], preferred_element_type=F32)
    u = jnp.dot(xn, wu_ref[...], preferred_element_type=F32)
    act = (g * jax.nn.sigmoid(g)) * u
    o_ref[...] += jnp.dot(act.astype(BF16), wo_ref[...], preferred_element_type=F32)


def ffn(h, norm_w, w_in, w_out):
    m, d = h.shape
    d_ff = w_out.shape[0]
    tm = _tile(m, 512)
    tf = _tile(d_ff, 256)
    nf = d_ff // tf
    return pl.pallas_call(
        _ffn_kernel,
        out_shape=jax.ShapeDtypeStruct((m, d), F32),
        grid=(m // tm, nf),
        in_specs=[
            pl.BlockSpec((tm, d), lambda i, f: (i, 0), pipeline_mode=pl.Buffered(1)),
            pl.BlockSpec((1, d), lambda i, f: (0, 0)),
            pl.BlockSpec((d, tf), lambda i, f: (0, f)),
            pl.BlockSpec((d, tf), lambda i, f: (0, f + nf)),
            pl.BlockSpec((tf, d), lambda i, f: (f, 0)),
        ],
        out_specs=pl.BlockSpec((tm, d), lambda i, f: (i, 0)),
        scratch_shapes=[pltpu.VMEM((tm, d), BF16)],
        compiler_params=_params("parallel", "arbitrary"),
        name="ffn",
    )(h, norm_w.reshape(1, d), w_in, w_in, w_out)


def _attn_prompt_kernel(sink_ref, q_ref, kc_ref, kp_ref, vc_ref, vp_ref, o_ref, *, n_kv, q_per_kv, hd, window):
    nb = pl.program_id(1)
    w = window
    rows = q_per_kv * w
    qi = lax.broadcasted_iota(jnp.int32, (rows, 2 * w), 0) & (w - 1)
    kj = lax.broadcasted_iota(jnp.int32, (rows, 2 * w), 1)
    no_prev = jnp.where(nb > 0, 0, 2 * w)
    valid = ((kj < w) & (kj > qi + no_prev)) | ((kj >= w) & ((kj - w) <= qi))
    scale = hd ** -0.5
    for h in range(n_kv):
        q = jnp.concatenate(
            [q_ref[:, (h * q_per_kv + g) * hd:(h * q_per_kv + g + 1) * hd] for g in range(q_per_kv)], axis=0)
        q = (q * scale).astype(BF16)
        ks = slice(h * hd, (h + 1) * hd)
        kk = jnp.concatenate([kp_ref[:, ks], kc_ref[:, ks]], axis=0).astype(BF16)
        vv = jnp.concatenate([vp_ref[:, ks], vc_ref[:, ks]], axis=0).astype(BF16)
        s = lax.dot_general(q, kk, (((1,), (1,)), ((), ())), preferred_element_type=F32)
        s = jnp.where(valid, s, -jnp.inf)
        sink = jnp.concatenate(
            [jnp.full((w, 1), sink_ref[h * q_per_kv + g], F32) for g in range(q_per_kv)], axis=0)
        m = jnp.maximum(jnp.max(s, axis=-1, keepdims=True), sink)
        e = jnp.exp(s - m)
        den = jnp.sum(e, axis=-1, keepdims=True) + jnp.exp(sink - m)
        o = jnp.dot(e.astype(BF16), vv, preferred_element_type=F32) * (1.0 / den)
        for g in range(q_per_kv):
            c = (h * q_per_kv + g) * hd
            o_ref[:, c:c + hd] = o[g * w:(g + 1) * w, :].astype(o_ref.dtype)


def attn_prompt(qkv, sinks, batch, seq, n_kv, q_per_kv, hd, window):
    q_dim = n_kv * q_per_kv * hd
    kv_dim = n_kv * hd
    nblk = seq // window
    kcol = q_dim // kv_dim
    cur = lambda b, n: b * nblk + n
    prev = lambda b, n: b * nblk + jnp.maximum(n - 1, 0)
    return pl.pallas_call(
        functools.partial(_attn_prompt_kernel, n_kv=n_kv, q_per_kv=q_per_kv, hd=hd, window=window),
        out_shape=jax.ShapeDtypeStruct((batch * seq, q_dim), BF16),
        grid=(batch, nblk),
        in_specs=[
            pl.BlockSpec(memory_space=pltpu.SMEM),
            pl.BlockSpec((window, q_dim), lambda b, n: (cur(b, n), 0)),
            pl.BlockSpec((window, kv_dim), lambda b, n: (cur(b, n), kcol)),
            pl.BlockSpec((window, kv_dim), lambda b, n: (prev(b, n), kcol)),
            pl.BlockSpec((window, kv_dim), lambda b, n: (cur(b, n), kcol + 1)),
            pl.BlockSpec((window, kv_dim), lambda b, n: (prev(b, n), kcol + 1)),
        ],
        out_specs=pl.BlockSpec((window, q_dim), lambda b, n: (cur(b, n), 0)),
        compiler_params=_params("parallel", "arbitrary"),
        name="attn_prompt",
    )(sinks, qkv, qkv, qkv, qkv, qkv)


def _attn_sample_kernel(sink_ref, q_ref, kn_ref, vn_ref, ck_ref, cv_ref, o_ref, wk_ref, wv_ref, *, n_kv, q_per_kv, hd,
                        window):
    w = window
    wk_ref[0, 0:w - 1, :] = ck_ref[0, 1:w, :]
    wk_ref[0, w - 1:w, :] = kn_ref[0]
    wv_ref[0, 0:w - 1, :] = cv_ref[0, 1:w, :]
    wv_ref[0, w - 1:w, :] = vn_ref[0]
    scale = hd ** -0.5
    for h in range(n_kv):
        rs = slice(h * q_per_kv, (h + 1) * q_per_kv)
        ks = slice(h * hd, (h + 1) * hd)
        q = (q_ref[0, rs, :] * scale).astype(BF16)
        kk = wk_ref[0, :, ks].astype(BF16)
        vv = wv_ref[0, :, ks].astype(BF16)
        s = lax.dot_general(q, kk, (((1,), (1,)), ((), ())), preferred_element_type=F32)
        sink = sink_ref[rs, :]
        m = jnp.maximum(jnp.max(s, axis=-1, keepdims=True), sink)
        e = jnp.exp(s - m)
        den = jnp.sum(e, axis=-1, keepdims=True) + jnp.exp(sink - m)
        o_ref[0, rs, :] = jnp.dot(e.astype(BF16), vv, preferred_element_type=F32) * (1.0 / den)


def attn_sample(q, k_new, v_new, cache_k, cache_v, sinks, n_kv, q_per_kv, hd, window):
    nb = q.shape[0]
    nq = n_kv * q_per_kv
    kv_dim = n_kv * hd
    win = pl.BlockSpec((1, window, kv_dim), lambda i: (i, 0, 0))
    row = pl.BlockSpec((1, 1, kv_dim), lambda i: (i, 0, 0))
    qs = pl.BlockSpec((1, nq, hd), lambda i: (i, 0, 0))
    return pl.pallas_call(
        functools.partial(_attn_sample_kernel, n_kv=n_kv, q_per_kv=q_per_kv, hd=hd, window=window),
        out_shape=(
            jax.ShapeDtypeStruct((nb, nq, hd), F32),
            jax.ShapeDtypeStruct((nb, window, kv_dim), F32),
            jax.ShapeDtypeStruct((nb, window, kv_dim), F32),
        ),
        grid=(nb,),
        in_specs=[pl.BlockSpec((nq, 1), lambda i: (0, 0)), qs, row, row, win, win],
        out_specs=(qs, win, win),
        compiler_params=_params("parallel"),
        name="attn_sample",
    )(sinks.reshape(nq, 1), q, k_new, v_new, cache_k, cache_v)


def _lower_bound(l, layer):
    e = jnp.exp(l - jnp.max(l, axis=0, keepdims=True))
    p = e / jnp.sum(e, axis=0, keepdims=True)
    return jnp.sum(p[0:layer + 1], axis=0, keepdims=True) - p[0:1]


def _cumsum_rows(x):
    n = x.shape[0]
    row = lax.broadcasted_iota(jnp.int32, x.shape, 0)
    s = 1
    while s < n:
        x = x + jnp.where(row >= s, pltpu.roll(x, s, axis=0), 0.0)
        s *= 2
    return x


def _block_ref_rows(b, half):
    c, k = b.shape
    blk = 2 * half
    if blk >= 8:
        b3 = b.reshape(c // blk, blk, k)
        return jnp.broadcast_to(b3[:, half - 1:half, :], (c // blk, blk, k)).reshape(c, k)
    b3 = b.reshape(c // 8, 8, k)
    sub = lax.broadcasted_iota(jnp.int32, (c // 8, 8, k), 1)
    out = None
    for j in range(8 // blk):
        r = j * blk + half - 1
        cand = jnp.broadcast_to(b3[:, r:r + 1, :], (c // 8, 8, k))
        out = cand if out is None else jnp.where(sub >= j * blk, cand, out)
    return out.reshape(c, k)


def _gla_prompt_kernel(lb_ref, nw_ref, q_ref, f_ref, i_ref, g_ref, og_ref, st_ref, s_scr, *, layer, n_heads, kd, vd):
    c = q_ref.shape[0]
    ci = pl.program_id(1)
    n_chunks = pl.num_programs(1)

    @pl.when(ci == 0)
    def _():
        s_scr[...] = jnp.zeros_like(s_scr)

    nw = nw_ref[...]
    ti = lax.broadcasted_iota(jnp.int32, (c, c), 0)
    si = lax.broadcasted_iota(jnp.int32, (c, c), 1)
    x = jnp.where(ti > si, ti ^ si, 0)
    level = jnp.full((c, c), -1, jnp.int32)
    half = 1
    lv = 0
    while half < c:
        level = jnp.where(x >= half, lv, level)
        half *= 2
        lv += 1
    n_levels = lv
    q_scale = kd ** -0.5

    def head(h, carry):
        ks = pl.ds(pl.multiple_of(h * kd, kd), kd)
        vs = pl.ds(pl.multiple_of(h * vd, vd), vd)
        lb = _lower_bound(lb_ref[:, ks], layer)
        qr = q_ref[:, ks]
        q = (qr * jax.nn.sigmoid(qr)) * q_scale
        forget = lb + (1.0 - lb) * jax.nn.sigmoid(f_ref[:, ks])
        k = 1.0 - forget
        v = i_ref[:, vs]
        b = _cumsum_rows(jnp.log(forget))
        b_last = b[c - 1:c, :]
        v16 = v.astype(BF16)
        s_t = s_scr[h]
        o = lax.dot_general((q * jnp.exp(b)).astype(BF16), s_t.astype(BF16), (((1,), (1,)), ((), ())),
                            preferred_element_type=F32)
        kdec = (k * jnp.exp(b_last - b)).astype(BF16)
        s_new = jnp.exp(b_last) * s_t + lax.dot_general(v16, kdec, (((0,), (0,)), ((), ())),
                                                       preferred_element_type=F32)
        s_scr[h] = s_new
        a = jnp.zeros((c, c), F32)
        hf = 1
        for l in range(n_levels):
            e = jnp.exp(-jnp.abs(b - _block_ref_rows(b, hf)))
            al = lax.dot_general((q * e).astype(BF16), (k * e).astype(BF16), (((1,), (1,)), ((), ())),
                                 preferred_element_type=F32)
            a = a + jnp.where(level == l, al, 0.0)
            hf *= 2
        o = o + jnp.dot(a.astype(BF16), v16, preferred_element_type=F32)
        o = o + jnp.sum(q * k, axis=-1, keepdims=True) * v
        o = (o * lax.rsqrt(jnp.mean(o * o, axis=-1, keepdims=True) + RMS_EPS)) * nw
        gr = g_ref[:, vs]
        og_ref[:, vs] = (o * (gr * jax.nn.sigmoid(gr))).astype(og_ref.dtype)

        @pl.when(ci == n_chunks - 1)
        def _():
            st_ref[0, h] = s_new.T

        return carry

    lax.fori_loop(0, n_heads, head, 0)


def gla_prompt(u, lower_bounds, norm_w, layer, batch, seq, n_heads, kd, vd):
    d = n_heads * kd
    c = _tile(seq, GLA_CHUNK)
    nch = seq // c
    n_layers = lower_bounds.shape[0]
    col = lambda j: pl.BlockSpec((c, d), lambda b, i: (b * nch + i, j))
    return pl.pallas_call(
        functools.partial(_gla_prompt_kernel, layer=layer, n_heads=n_heads, kd=kd, vd=vd),
        out_shape=(
            jax.ShapeDtypeStruct((batch * seq, d), BF16),
            jax.ShapeDtypeStruct((batch, n_heads, kd, vd), F32),
        ),
        grid=(batch, nch),
        in_specs=[
            pl.BlockSpec((n_layers, d), lambda b, i: (0, 0)),
            pl.BlockSpec((1, vd), lambda b, i: (0, 0)),
            col(0), col(1), col(2), col(3),
        ],
        out_specs=(
            pl.BlockSpec((c, d), lambda b, i: (b * nch + i, 0)),
            pl.BlockSpec((1, n_heads, kd, vd), lambda b, i: (b, 0, 0, 0)),
        ),
        scratch_shapes=[pltpu.VMEM((n_heads, vd, kd), F32)],
        compiler_params=_params("parallel", "arbitrary"),
        name="gla_prompt",
    )(lower_bounds, norm_w.reshape(1, vd), u, u, u, u)


def _gla_sample_kernel(lb_ref, nw_ref, x_ref, s_ref, og_ref, so_ref, *, layer, n_heads, kd):
    nh = n_heads
    x = x_ref[0]
    lb = _lower_bound(lb_ref[...], layer)[0]
    qr = x[0:nh]
    q = (qr * jax.nn.sigmoid(qr)) * (kd ** -0.5)
    forget = lb + (1.0 - lb) * jax.nn.sigmoid(x[nh:2 * nh])
    k = 1.0 - forget
    v = x[2 * nh:3 * nh]
    gr = x[3 * nh:4 * nh]
    pad = jnp.zeros((x.shape[0] - 3 * nh, kd), F32)
    cols = jnp.concatenate([q, forget, k, pad], axis=0).T
    outs = []
    for h in range(nh):
        s_new = cols[:, nh + h:nh + h + 1] * s_ref[0, h] + cols[:, 2 * nh + h:2 * nh + h + 1] * v[h:h + 1, :]
        so_ref[0, h] = s_new
        outs.append(jnp.sum(cols[:, h:h + 1] * s_new, axis=0, keepdims=True))
    o = jnp.concatenate(outs, axis=0)
    o = (o * lax.rsqrt(jnp.mean(o * o, axis=-1, keepdims=True) + RMS_EPS)) * nw_ref[...]
    og_ref[0] = o * (gr * jax.nn.sigmoid(gr))


def gla_sample(u, state, lower_bounds, norm_w, layer, n_heads, kd, vd):
    nb = u.shape[0]
    n_layers = lower_bounds.shape[0]
    rows = 4 * n_heads
    return pl.pallas_call(
        functools.partial(_gla_sample_kernel, layer=layer, n_heads=n_heads, kd=kd),
        out_shape=(
            jax.ShapeDtypeStruct((nb, n_heads, vd), F32),
            jax.ShapeDtypeStruct((nb, n_heads, kd, vd), F32),
        ),
        grid=(nb,),
        in_specs=[
            pl.BlockSpec((n_layers, n_heads, kd), lambda i: (0, 0, 0)),
            pl.BlockSpec((1, vd), lambda i: (0, 0)),
            pl.BlockSpec((1, rows, kd), lambda i: (i, 0, 0)),
            pl.BlockSpec((1, n_heads, kd, vd), lambda i: (i, 0, 0, 0)),
        ],
        out_specs=(
            pl.BlockSpec((1, n_heads, vd), lambda i: (i, 0, 0)),
            pl.BlockSpec((1, n_heads, kd, vd), lambda i: (i, 0, 0, 0)),
        ),
        compiler_params=_params("parallel"),
        name="gla_sample",
    )(lower_bounds.reshape(n_layers, n_heads, kd), norm_w.reshape(1, vd), u.reshape(nb, rows, kd), state)


def kernel(x_prompt, x_sample, cache_k, cache_v, state_hgrn, norm_mix, norm_ffn, norm_final, attn_w_qkv, attn_b_qkv,
           attn_sinks, attn_w_o, attn_b_o, hgrn_w_in, hgrn_lower_bounds, hgrn_norm, hgrn_w_o, ffn_w_in, ffn_w_out):
    batch, seq, d = x_prompt.shape
    nb = x_sample.shape[0]
    assert x_sample.shape[1] == 1
    depth = norm_mix.shape[0]
    window, n_kv, hd = cache_k.shape[2:]
    n_q = attn_sinks.shape[1]
    q_per_kv = n_q // n_kv
    q_dim, kv_dim = n_q * hd, n_kv * hd
    n_heads, kd, vd = state_hgrn.shape[2:]

    hp = x_prompt.reshape(batch * seq, d)
    hs = x_sample.reshape(nb, d)
    wk_p, wv_p, st_p, wk_s, wv_s, st_s = [], [], [], [], [], []
    for i in range(depth):
        xp = rmsnorm(hp, norm_mix[i], BF16)
        xs = rmsnorm(hs, norm_mix[i], BF16)
        if i % 2 == 0:
            a = i // 2
            w_qkv = attn_w_qkv[a].astype(BF16)
            w_o = attn_w_o[a].astype(BF16)
            qkv_p = matmul(xp, w_qkv, bias=attn_b_qkv[a])
            op = attn_prompt(qkv_p, attn_sinks[a], batch, seq, n_kv, q_per_kv, hd, window)
            hp = matmul(op, w_o, bias=attn_b_o[a], res=hp)
            kv_p = qkv_p.reshape(batch, seq, q_dim + 2 * kv_dim)[:, seq - window:, q_dim:]
            wk_p.append(kv_p[..., :kv_dim].reshape(batch, window, n_kv, hd))
            wv_p.append(kv_p[..., kv_dim:].reshape(batch, window, n_kv, hd))

            qkv_s = matmul(xs, w_qkv, bias=attn_b_qkv[a])
            os_, wk, wv = attn_sample(
                qkv_s[:, :q_dim].reshape(nb, n_q, hd),
                qkv_s[:, q_dim:q_dim + kv_dim].reshape(nb, 1, kv_dim),
                qkv_s[:, q_dim + kv_dim:].reshape(nb, 1, kv_dim),
                cache_k[a].reshape(nb, window, kv_dim), cache_v[a].reshape(nb, window, kv_dim),
                attn_sinks[a], n_kv, q_per_kv, hd, window)
            hs = matmul(os_.reshape(nb, q_dim).astype(BF16), w_o, bias=attn_b_o[a], res=hs)
            wk_s.append(wk.reshape(nb, window, n_kv, hd))
            wv_s.append(wv.reshape(nb, window, n_kv, hd))
        else:
            r = i // 2
            w_in = hgrn_w_in[r].astype(BF16)
            w_o = hgrn_w_o[r].astype(BF16)
            up = matmul(xp, w_in)
            ogp, sp = gla_prompt(up, hgrn_lower_bounds, hgrn_norm[r], i, batch, seq, n_heads, kd, vd)
            hp = matmul(ogp, w_o, res=hp)
            st_p.append(sp)

            us = matmul(xs, w_in)
            ogs, ss = gla_sample(us, state_hgrn[r], hgrn_lower_bounds, hgrn_norm[r], i, n_heads, kd, vd)
            hs = matmul(ogs.reshape(nb, d).astype(BF16), w_o, res=hs)
            st_s.append(ss)
        w_ffn_in = ffn_w_in[i].astype(BF16)
        w_ffn_out = ffn_w_out[i].astype(BF16)
        hp = ffn(hp, norm_ffn[i], w_ffn_in, w_ffn_out)
        hs = ffn(hs, norm_ffn[i], w_ffn_in, w_ffn_out)

    y_prompt = rmsnorm(hp, norm_final, F32).reshape(batch, seq, d)
    y_sample = rmsnorm(hs, norm_final, F32).reshape(nb, 1, d)
    return (y_prompt, y_sample, jnp.stack(wk_p), jnp.stack(wv_p), jnp.stack(st_p), jnp.stack(wk_s), jnp.stack(wv_s),
            jnp.stack(st_s))
```

```python
import functools

import jax
import jax.numpy as jnp
from jax import lax
from jax.experimental import pallas as pl
from jax.experimental.pallas import tpu as pltpu

RMS_EPS = 1e-5
BF16 = jnp.bfloat16
F32 = jnp.float32

V7X_VMEM_LIMIT_BYTES = 56 * 1024 * 1024
GLA_CHUNK = 128


def _params(*sem):
    return pltpu.CompilerParams(dimension_semantics=sem, vmem_limit_bytes=V7X_VMEM_LIMIT_BYTES)


def _tile(n, pref):
    return pref if n % pref == 0 else n


def _rmsnorm_rows(x, w):
    ms = jnp.mean(x * x, axis=-1, keepdims=True)
    return (x * lax.rsqrt(ms + RMS_EPS)) * w


def _rmsnorm_kernel(x_ref, w_ref, o_ref):
    o_ref[...] = _rmsnorm_rows(x_ref[...], w_ref[...]).astype(o_ref.dtype)


def rmsnorm(x, w, out_dtype):
    m, d = x.shape
    tm = _tile(m, 256)
    return pl.pallas_call(
        _rmsnorm_kernel,
        out_shape=jax.ShapeDtypeStruct((m, d), out_dtype),
        grid=(m // tm,),
        in_specs=[pl.BlockSpec((tm, d), lambda i: (i, 0)), pl.BlockSpec((1, d), lambda i: (0, 0))],
        out_specs=pl.BlockSpec((tm, d), lambda i: (i, 0)),
        compiler_params=_params("parallel"),
        name="rmsnorm",
    )(x, w.reshape(1, d))


def _matmul_kernel(*refs, has_bias, has_res):
    x_ref, xs_ref, w_ref = refs[0], refs[1], refs[2]
    pos = 3
    b_ref = r_ref = rs_ref = None
    if has_bias:
        b_ref = refs[pos]
        pos += 1
    if has_res:
        r_ref, rs_ref = refs[pos], refs[pos + 1]
    o_ref, os_ref = refs[-2], refs[-1]
    w = w_ref[...].astype(BF16)

    def finish(acc, res_ref, out_ref):
        if has_bias:
            acc = acc + b_ref[...]
        if has_res:
            acc = acc + res_ref[...]
        out_ref[...] = acc.astype(out_ref.dtype)

    finish(jnp.dot(x_ref[...], w, preferred_element_type=F32), r_ref, o_ref)

    @pl.when(pl.program_id(0) == 0)
    def _():
        finish(jnp.dot(xs_ref[...], w, preferred_element_type=F32), rs_ref, os_ref)


def matmul(x, xs, w, layer, bias=None, res=None, res_s=None, out_dtype=F32):
    m, k = x.shape
    ms = xs.shape[0]
    n = w.shape[2]
    tm = _tile(m, 1024)
    tn = _tile(n, 512)
    nj = n // tn
    js = lambda i, j: jnp.where(i == 0, j, nj - 1)
    in_specs = [
        pl.BlockSpec((tm, k), lambda i, j: (i, 0)),
        pl.BlockSpec((ms, k), lambda i, j: (0, 0)),
        pl.BlockSpec((None, k, tn), lambda i, j: (layer, 0, j)),
    ]
    args = [x, xs, w]
    if bias is not None:
        in_specs.append(pl.BlockSpec((1, tn), lambda i, j: (0, j)))
        args.append(bias.reshape(1, n))
    if res is not None:
        in_specs.append(pl.BlockSpec((tm, tn), lambda i, j: (i, j)))
        in_specs.append(pl.BlockSpec((ms, tn), lambda i, j: (0, js(i, j))))
        args += [res, res_s]
    return pl.pallas_call(
        functools.partial(_matmul_kernel, has_bias=bias is not None, has_res=res is not None),
        out_shape=(jax.ShapeDtypeStruct((m, n), out_dtype), jax.ShapeDtypeStruct((ms, n), out_dtype)),
        grid=(m // tm, nj),
        in_specs=in_specs,
        out_specs=(pl.BlockSpec((tm, tn), lambda i, j: (i, j)), pl.BlockSpec((ms, tn), lambda i, j: (0, js(i, j)))),
        compiler_params=_params("arbitrary", "arbitrary"),
        name="matmul",
    )(*args)


FFN_ROWS = 1024
FFN_OUT_COLS = 512
FFN_NORM_ROWS = 32


def _ffn_kernel(h_ref, nw_ref, wg_ref, wu_ref, wo_ref, o_ref, xn_ref):
    f = pl.program_id(1)
    d = o_ref.shape[1]

    @pl.when(f == 0)
    def _():
        rows = min(FFN_NORM_ROWS, h_ref.shape[0])

        def norm_rows(r, carry):
            rs = pl.ds(pl.multiple_of(r * rows, rows), rows)
            h = h_ref[rs, :]
            xn_ref[rs, :] = _rmsnorm_rows(h, nw_ref[...]).astype(BF16)
            o_ref[rs, :] = h
            return carry

        lax.fori_loop(0, h_ref.shape[0] // rows, norm_rows, 0)

    xn = xn_ref[...]
    g = jnp.dot(xn, wg_ref[...], preferred_element_type=F32)
    u = jnp.dot(xn, wu_ref[...], preferred_element_type=F32)
    act = ((g * jax.nn.sigmoid(g)) * u).astype(BF16)
    for c in range(0, d, FFN_OUT_COLS):
        cs = slice(c, min(c + FFN_OUT_COLS, d))
        o_ref[:, cs] += jnp.dot(act, wo_ref[:, cs], preferred_element_type=F32)


def ffn(h, norm_w, w_in, w_out, layer):
    m, d = h.shape
    d_ff = w_out.shape[1]
    tm = _tile(m, FFN_ROWS)
    tf = _tile(d_ff, 256)
    nf = d_ff // tf
    once = pl.Buffered(1)
    return pl.pallas_call(
        _ffn_kernel,
        out_shape=jax.ShapeDtypeStruct((m, d), F32),
        grid=(m // tm, nf),
        in_specs=[
            pl.BlockSpec((tm, d), lambda i, f: (i, 0), pipeline_mode=once),
            pl.BlockSpec((1, d), lambda i, f: (0, 0), pipeline_mode=once),
            pl.BlockSpec((None, d, tf), lambda i, f: (layer, 0, f)),
            pl.BlockSpec((None, d, tf), lambda i, f: (layer, 0, f + nf)),
            pl.BlockSpec((None, tf, d), lambda i, f: (layer, f, 0)),
        ],
        out_specs=pl.BlockSpec((tm, d), lambda i, f: (i, 0), pipeline_mode=once),
        scratch_shapes=[pltpu.VMEM((tm, d), BF16)],
        compiler_params=_params("parallel", "arbitrary"),
        name="ffn",
    )(h, norm_w.reshape(1, d), w_in, w_in, w_out)


def _attn_prompt_kernel(sink_ref, q_ref, kc_ref, kp_ref, vc_ref, vp_ref, o_ref, *, n_kv, q_per_kv, hd, window):
    nb = pl.program_id(1)
    w = window
    rows = q_per_kv * w
    qi = lax.broadcasted_iota(jnp.int32, (rows, 2 * w), 0) & (w - 1)
    kj = lax.broadcasted_iota(jnp.int32, (rows, 2 * w), 1)
    no_prev = jnp.where(nb > 0, 0, 2 * w)
    valid = ((kj < w) & (kj > qi + no_prev)) | ((kj >= w) & ((kj - w) <= qi))
    scale = hd ** -0.5
    for h in range(n_kv):
        q = jnp.concatenate(
            [q_ref[:, (h * q_per_kv + g) * hd:(h * q_per_kv + g + 1) * hd] for g in range(q_per_kv)], axis=0)
        q = (q * scale).astype(BF16)
        ks = slice(h * hd, (h + 1) * hd)
        kk = jnp.concatenate([kp_ref[:, ks], kc_ref[:, ks]], axis=0).astype(BF16)
        vv = jnp.concatenate([vp_ref[:, ks], vc_ref[:, ks]], axis=0).astype(BF16)
        s = lax.dot_general(q, kk, (((1,), (1,)), ((), ())), preferred_element_type=F32)
        s = jnp.where(valid, s, -jnp.inf)
        sink = jnp.concatenate(
            [jnp.full((w, 1), sink_ref[h * q_per_kv + g], F32) for g in range(q_per_kv)], axis=0)
        m = jnp.maximum(jnp.max(s, axis=-1, keepdims=True), sink)
        e = jnp.exp(s - m)
        den = jnp.sum(e, axis=-1, keepdims=True) + jnp.exp(sink - m)
        o = jnp.dot(e.astype(BF16), vv, preferred_element_type=F32) * (1.0 / den)
        for g in range(q_per_kv):
            c = (h * q_per_kv + g) * hd
            o_ref[:, c:c + hd] = o[g * w:(g + 1) * w, :].astype(o_ref.dtype)


def attn_prompt(qkv, sinks, batch, seq, n_kv, q_per_kv, hd, window):
    q_dim = n_kv * q_per_kv * hd
    kv_dim = n_kv * hd
    nblk = seq // window
    kcol = q_dim // kv_dim
    cur = lambda b, n: b * nblk + n
    prev = lambda b, n: b * nblk + jnp.maximum(n - 1, 0)
    return pl.pallas_call(
        functools.partial(_attn_prompt_kernel, n_kv=n_kv, q_per_kv=q_per_kv, hd=hd, window=window),
        out_shape=jax.ShapeDtypeStruct((batch * seq, q_dim), BF16),
        grid=(batch, nblk),
        in_specs=[
            pl.BlockSpec(memory_space=pltpu.SMEM),
            pl.BlockSpec((window, q_dim), lambda b, n: (cur(b, n), 0)),
            pl.BlockSpec((window, kv_dim), lambda b, n: (cur(b, n), kcol)),
            pl.BlockSpec((window, kv_dim), lambda b, n: (prev(b, n), kcol)),
            pl.BlockSpec((window, kv_dim), lambda b, n: (cur(b, n), kcol + 1)),
            pl.BlockSpec((window, kv_dim), lambda b, n: (prev(b, n), kcol + 1)),
        ],
        out_specs=pl.BlockSpec((window, q_dim), lambda b, n: (cur(b, n), 0)),
        compiler_params=_params("parallel", "arbitrary"),
        name="attn_prompt",
    )(sinks, qkv, qkv, qkv, qkv, qkv)


def _attn_sample_kernel(sink_ref, q_ref, kn_ref, vn_ref, ck_ref, cv_ref, o_ref, wk_ref, wv_ref, *, n_kv, q_per_kv, hd,
                        window):
    w = window
    wk_ref[0, 0:w - 1, :] = ck_ref[0, 1:w, :]
    wk_ref[0, w - 1:w, :] = kn_ref[0]
    wv_ref[0, 0:w - 1, :] = cv_ref[0, 1:w, :]
    wv_ref[0, w - 1:w, :] = vn_ref[0]
    scale = hd ** -0.5
    for h in range(n_kv):
        rs = slice(h * q_per_kv, (h + 1) * q_per_kv)
        ks = slice(h * hd, (h + 1) * hd)
        q = (q_ref[0, rs, :] * scale).astype(BF16)
        kk = wk_ref[0, :, ks].astype(BF16)
        vv = wv_ref[0, :, ks].astype(BF16)
        s = lax.dot_general(q, kk, (((1,), (1,)), ((), ())), preferred_element_type=F32)
        sink = sink_ref[rs, :]
        m = jnp.maximum(jnp.max(s, axis=-1, keepdims=True), sink)
        e = jnp.exp(s - m)
        den = jnp.sum(e, axis=-1, keepdims=True) + jnp.exp(sink - m)
        o_ref[0, rs, :] = jnp.dot(e.astype(BF16), vv, preferred_element_type=F32) * (1.0 / den)


def attn_sample(q, k_new, v_new, cache_k, cache_v, sinks, n_kv, q_per_kv, hd, window):
    nb = q.shape[0]
    nq = n_kv * q_per_kv
    kv_dim = n_kv * hd
    win = pl.BlockSpec((1, window, kv_dim), lambda i: (i, 0, 0))
    row = pl.BlockSpec((1, 1, kv_dim), lambda i: (i, 0, 0))
    qs = pl.BlockSpec((1, nq, hd), lambda i: (i, 0, 0))
    return pl.pallas_call(
        functools.partial(_attn_sample_kernel, n_kv=n_kv, q_per_kv=q_per_kv, hd=hd, window=window),
        out_shape=(
            jax.ShapeDtypeStruct((nb, nq, hd), F32),
            jax.ShapeDtypeStruct((nb, window, kv_dim), F32),
            jax.ShapeDtypeStruct((nb, window, kv_dim), F32),
        ),
        grid=(nb,),
        in_specs=[pl.BlockSpec((nq, 1), lambda i: (0, 0)), qs, row, row, win, win],
        out_specs=(qs, win, win),
        compiler_params=_params("parallel"),
        name="attn_sample",
    )(sinks.reshape(nq, 1), q, k_new, v_new, cache_k, cache_v)


def _lower_bound(l, layer):
    e = jnp.exp(l - jnp.max(l, axis=0, keepdims=True))
    p = e / jnp.sum(e, axis=0, keepdims=True)
    return jnp.sum(p[0:layer + 1], axis=0, keepdims=True) - p[0:1]


def _cumsum_rows(x, tri):
    k = x.shape[1]
    hi = x.astype(BF16)
    r1 = x - hi.astype(F32)
    mid = r1.astype(BF16)
    lo = (r1 - mid.astype(F32)).astype(BF16)
    parts = jnp.dot(tri, jnp.concatenate([hi, mid, lo], axis=1), preferred_element_type=F32)
    return (parts[:, 0:k] + parts[:, k:2 * k]) + parts[:, 2 * k:3 * k]


def _block_ref_rows(b, half):
    c, k = b.shape
    blk = 2 * half
    if blk >= 8:
        b3 = b.reshape(c // blk, blk, k)
        return jnp.broadcast_to(b3[:, half - 1:half, :], (c // blk, blk, k)).reshape(c, k)
    b3 = b.reshape(c // 8, 8, k)
    sub = lax.broadcasted_iota(jnp.int32, (c // 8, 8, k), 1)
    out = None
    for j in range(8 // blk):
        r = j * blk + half - 1
        cand = jnp.broadcast_to(b3[:, r:r + 1, :], (c // 8, 8, k))
        out = cand if out is None else jnp.where(sub >= j * blk, cand, out)
    return out.reshape(c, k)


def _gla_prompt_kernel(lb_ref, nw_ref, q_ref, f_ref, i_ref, g_ref, og_ref, st_ref, s_scr, *, layer, n_heads, kd, vd):
    c = q_ref.shape[0]
    ci = pl.program_id(1)
    n_chunks = pl.num_programs(1)

    @pl.when(ci == 0)
    def _():
        s_scr[...] = jnp.zeros_like(s_scr)

    nw = nw_ref[...]
    ti = lax.broadcasted_iota(jnp.int32, (c, c), 0)
    si = lax.broadcasted_iota(jnp.int32, (c, c), 1)
    pair_bits = jnp.where(ti > si, ti ^ si, 0)
    tri = (ti >= si).astype(BF16)
    n_levels = c.bit_length() - 1
    q_scale = kd ** -0.5

    def head(h, carry):
        ks = pl.ds(pl.multiple_of(h * kd, kd), kd)
        vs = pl.ds(pl.multiple_of(h * vd, vd), vd)
        lb = _lower_bound(lb_ref[:, ks], layer)
        qr = q_ref[:, ks]
        q = (qr * jax.nn.sigmoid(qr)) * q_scale
        forget = lb + (1.0 - lb) * jax.nn.sigmoid(f_ref[:, ks])
        k = 1.0 - forget
        v = i_ref[:, vs]
        b = _cumsum_rows(jnp.log(forget), tri)
        b_last = b[c - 1:c, :]
        v16 = v.astype(BF16)
        s_t = s_scr[h]
        o = lax.dot_general((q * jnp.exp(b)).astype(BF16), s_t.astype(BF16), (((1,), (1,)), ((), ())),
                            preferred_element_type=F32)
        kdec = (k * jnp.exp(b_last - b)).astype(BF16)
        s_new = jnp.exp(b_last) * s_t + lax.dot_general(v16, kdec, (((0,), (0,)), ((), ())),
                                                       preferred_element_type=F32)
        s_scr[h] = s_new
        a = jnp.zeros((c, c), F32)
        hf = 1
        for l in range(n_levels):
            e = jnp.exp(-jnp.abs(b - _block_ref_rows(b, hf)))
            al = lax.dot_general((q * e).astype(BF16), (k * e).astype(BF16), (((1,), (1,)), ((), ())),
                                 preferred_element_type=F32)
            a = jnp.where(pair_bits >= hf, al, a)
            hf *= 2
        o = o + jnp.dot(a.astype(BF16), v16, preferred_element_type=F32)
        o = o + jnp.sum(q * k, axis=-1, keepdims=True) * v
        o = (o * lax.rsqrt(jnp.mean(o * o, axis=-1, keepdims=True) + RMS_EPS)) * nw
        gr = g_ref[:, vs]
        og_ref[:, vs] = (o * (gr * jax.nn.sigmoid(gr))).astype(og_ref.dtype)
        return carry

    lax.fori_loop(0, n_heads, head, 0, unroll=4)

    @pl.when(ci == n_chunks - 1)
    def _():
        def emit(h, carry):
            st_ref[0, h] = s_scr[h].T
            return carry

        lax.fori_loop(0, n_heads, emit, 0)


def gla_prompt(u, lower_bounds, norm_w, layer, batch, seq, n_heads, kd, vd):
    d = n_heads * kd
    c = _tile(seq, GLA_CHUNK)
    nch = seq // c
    n_layers = lower_bounds.shape[0]
    col = lambda j: pl.BlockSpec((c, d), lambda b, i: (b * nch + i, j))
    return pl.pallas_call(
        functools.partial(_gla_prompt_kernel, layer=layer, n_heads=n_heads, kd=kd, vd=vd),
        out_shape=(
            jax.ShapeDtypeStruct((batch * seq, d), BF16),
            jax.ShapeDtypeStruct((batch, n_heads, kd, vd), F32),
        ),
        grid=(batch, nch),
        in_specs=[
            pl.BlockSpec((n_layers, d), lambda b, i: (0, 0)),
            pl.BlockSpec((1, vd), lambda b, i: (0, 0)),
            col(0), col(1), col(2), col(3),
        ],
        out_specs=(
            pl.BlockSpec((c, d), lambda b, i: (b * nch + i, 0)),
            pl.BlockSpec((1, n_heads, kd, vd), lambda b, i: (b, 0, 0, 0)),
        ),
        scratch_shapes=[pltpu.VMEM((n_heads, vd, kd), F32)],
        compiler_params=_params("parallel", "arbitrary"),
        name="gla_prompt",
    )(lower_bounds, norm_w.reshape(1, vd), u, u, u, u)


def _gla_sample_kernel(lb_ref, nw_ref, x_ref, s_ref, og_ref, so_ref, *, layer, n_heads, kd):
    nh = n_heads
    x = x_ref[0]
    lb = _lower_bound(lb_ref[...], layer)[0]
    qr = x[0:nh]
    q = (qr * jax.nn.sigmoid(qr)) * (kd ** -0.5)
    forget = lb + (1.0 - lb) * jax.nn.sigmoid(x[nh:2 * nh])
    k = 1.0 - forget
    v = x[2 * nh:3 * nh]
    gr = x[3 * nh:4 * nh]
    pad = jnp.zeros((x.shape[0] - 3 * nh, kd), F32)
    cols = jnp.concatenate([q, forget, k, pad], axis=0).T
    outs = []
    for h in range(nh):
        s_new = cols[:, nh + h:nh + h + 1] * s_ref[0, h] + cols[:, 2 * nh + h:2 * nh + h + 1] * v[h:h + 1, :]
        so_ref[0, h] = s_new
        outs.append(jnp.sum(cols[:, h:h + 1] * s_new, axis=0, keepdims=True))
    o = jnp.concatenate(outs, axis=0)
    o = (o * lax.rsqrt(jnp.mean(o * o, axis=-1, keepdims=True) + RMS_EPS)) * nw_ref[...]
    og_ref[0] = o * (gr * jax.nn.sigmoid(gr))


def gla_sample(u, state, lower_bounds, norm_w, layer, n_heads, kd, vd):
    nb = u.shape[0]
    n_layers = lower_bounds.shape[0]
    rows = 4 * n_heads
    return pl.pallas_call(
        functools.partial(_gla_sample_kernel, layer=layer, n_heads=n_heads, kd=kd),
        out_shape=(
            jax.ShapeDtypeStruct((nb, n_heads, vd), F32),
            jax.ShapeDtypeStruct((nb, n_heads, kd, vd), F32),
        ),
        grid=(nb,),
        in_specs=[
            pl.BlockSpec((n_layers, n_heads, kd), lambda i: (0, 0, 0)),
            pl.BlockSpec((1, vd), lambda i: (0, 0)),
            pl.BlockSpec((1, rows, kd), lambda i: (i, 0, 0)),
            pl.BlockSpec((1, n_heads, kd, vd), lambda i: (i, 0, 0, 0)),
        ],
        out_specs=(
            pl.BlockSpec((1, n_heads, vd), lambda i: (i, 0, 0)),
            pl.BlockSpec((1, n_heads, kd, vd), lambda i: (i, 0, 0, 0)),
        ),
        compiler_params=_params("parallel"),
        name="gla_sample",
    )(lower_bounds.reshape(n_layers, n_heads, kd), norm_w.reshape(1, vd), u.reshape(nb, rows, kd), state)


def kernel(x_prompt, x_sample, cache_k, cache_v, state_hgrn, norm_mix, norm_ffn, norm_final, attn_w_qkv, attn_b_qkv,
           attn_sinks, attn_w_o, attn_b_o, hgrn_w_in, hgrn_lower_bounds, hgrn_norm, hgrn_w_o, ffn_w_in, ffn_w_out):
    batch, seq, d = x_prompt.shape
    nb = x_sample.shape[0]
    assert x_sample.shape[1] == 1
    depth = norm_mix.shape[0]
    window, n_kv, hd = cache_k.shape[2:]
    n_q = attn_sinks.shape[1]
    q_per_kv = n_q // n_kv
    q_dim, kv_dim = n_q * hd, n_kv * hd
    n_heads, kd, vd = state_hgrn.shape[2:]

    hp = x_prompt.reshape(batch * seq, d)
    hs = x_sample.reshape(nb, d)
    wk_p, wv_p, st_p, wk_s, wv_s, st_s = [], [], [], [], [], []
    w_ffn_in = ffn_w_in.astype(BF16)
    w_ffn_out = ffn_w_out.astype(BF16)
    for i in range(depth):
        xp = rmsnorm(hp, norm_mix[i], BF16)
        xs = rmsnorm(hs, norm_mix[i], BF16)
        if i % 2 == 0:
            a = i // 2
            qkv_p, qkv_s = matmul(xp, xs, attn_w_qkv, a, bias=attn_b_qkv[a])
            op = attn_prompt(qkv_p, attn_sinks[a], batch, seq, n_kv, q_per_kv, hd, window)
            kv_p = qkv_p.reshape(batch, seq, q_dim + 2 * kv_dim)[:, seq - window:, q_dim:]
            wk_p.append(kv_p[..., :kv_dim].reshape(batch, window, n_kv, hd))
            wv_p.append(kv_p[..., kv_dim:].reshape(batch, window, n_kv, hd))
            os_, wk, wv = attn_sample(
                qkv_s[:, :q_dim].reshape(nb, n_q, hd),
                qkv_s[:, q_dim:q_dim + kv_dim].reshape(nb, 1, kv_dim),
                qkv_s[:, q_dim + kv_dim:].reshape(nb, 1, kv_dim),
                cache_k[a].reshape(nb, window, kv_dim), cache_v[a].reshape(nb, window, kv_dim),
                attn_sinks[a], n_kv, q_per_kv, hd, window)
            wk_s.append(wk.reshape(nb, window, n_kv, hd))
            wv_s.append(wv.reshape(nb, window, n_kv, hd))
            hp, hs = matmul(op, os_.reshape(nb, q_dim).astype(BF16), attn_w_o, a, bias=attn_b_o[a], res=hp, res_s=hs)
        else:
            r = i // 2
            up, us = matmul(xp, xs, hgrn_w_in, r)
            ogp, sp = gla_prompt(up, hgrn_lower_bounds, hgrn_norm[r], i, batch, seq, n_heads, kd, vd)
            ogs, ss = gla_sample(us, state_hgrn[r], hgrn_lower_bounds, hgrn_norm[r], i, n_heads, kd, vd)
            st_p.append(sp)
            st_s.append(ss)
            hp, hs = matmul(ogp, ogs.reshape(nb, d).astype(BF16), hgrn_w_o, r, res=hp, res_s=hs)
        hp = ffn(hp, norm_ffn[i], w_ffn_in, w_ffn_out, i)
        hs = ffn(hs, norm_ffn[i], w_ffn_in, w_ffn_out, i)

    y_prompt = rmsnorm(hp, norm_final, F32).reshape(batch, seq, d)
    y_sample = rmsnorm(hs, norm_final, F32).reshape(nb, 1, d)
    return (y_prompt, y_sample, jnp.stack(wk_p), jnp.stack(wv_p), jnp.stack(st_p), jnp.stack(wk_s), jnp.stack(wv_s),
            jnp.stack(st_s))
```

```python
import functools

import jax
import jax.numpy as jnp
from jax import lax
from jax.experimental import pallas as pl
from jax.experimental.pallas import tpu as pltpu

RMS_EPS = 1e-5
BF16 = jnp.bfloat16
F32 = jnp.float32

V7X_VMEM_LIMIT_BYTES = 56 * 1024 * 1024
GLA_CHUNK = 128
SUBLANES = 8
LANES = 128
LOG2E = 1.4426950408889634


def _params(*sem):
    return pltpu.CompilerParams(dimension_semantics=sem, vmem_limit_bytes=V7X_VMEM_LIMIT_BYTES)


def _tile(n, pref):
    return pref if n % pref == 0 else n


def _rmsnorm_rows(x, w):
    ms = jnp.mean(x * x, axis=-1, keepdims=True)
    return (x * lax.rsqrt(ms + RMS_EPS)) * w


NORM_ROWS = 32


def _norm_rows_into(src_ref, nw_ref, dst_ref, copy_ref=None):
    total = src_ref.shape[0]
    rows = min(NORM_ROWS, total)

    def body(r, carry):
        rs = pl.ds(pl.multiple_of(r * rows, rows), rows)
        h = src_ref[rs, :]
        dst_ref[rs, :] = _rmsnorm_rows(h, nw_ref[...]).astype(BF16)
        if copy_ref is not None:
            copy_ref[rs, :] = h
        return carry

    lax.fori_loop(0, total // rows, body, 0)


def _matmul_kernel(*refs, has_norm, has_bias, has_res):
    x_ref, xs_ref, w_ref = refs[0], refs[1], refs[2]
    pos = 3
    nw_ref = b_ref = r_ref = rs_ref = None
    if has_norm:
        nw_ref = refs[pos]
        pos += 1
    if has_bias:
        b_ref = refs[pos]
        pos += 1
    if has_res:
        r_ref, rs_ref = refs[pos], refs[pos + 1]
        pos += 2
    o_ref, os_ref = refs[pos], refs[pos + 1]
    i, j = pl.program_id(0), pl.program_id(1)
    if has_norm:
        xn_ref, xsn_ref = refs[pos + 2], refs[pos + 3]

        @pl.when(j == 0)
        def _():
            _norm_rows_into(x_ref, nw_ref, xn_ref)

        @pl.when((j == 0) & (i == 0))
        def _():
            _norm_rows_into(xs_ref, nw_ref, xsn_ref)
    else:
        xn_ref, xsn_ref = x_ref, xs_ref
    w = w_ref[...].astype(BF16)

    def finish(acc, res_ref, out_ref):
        if has_bias:
            acc = acc + b_ref[...]
        if has_res:
            acc = acc + res_ref[...]
        out_ref[...] = acc.astype(out_ref.dtype)

    finish(jnp.dot(xn_ref[...], w, preferred_element_type=F32), r_ref, o_ref)

    @pl.when(i == 0)
    def _():
        finish(jnp.dot(xsn_ref[...], w, preferred_element_type=F32), rs_ref, os_ref)


def matmul(x, xs, w, layer, norm_w=None, bias=None, res=None, res_s=None, out_dtype=F32):
    m, k = x.shape
    ms = xs.shape[0]
    n = w.shape[2]
    tm = _tile(m, 1024)
    tn = _tile(n, 512)
    nj = n // tn
    js = lambda i, j: jnp.where(i == 0, j, nj - 1)
    has_norm = norm_w is not None
    once = pl.Buffered(1)
    in_specs = [
        pl.BlockSpec((tm, k), lambda i, j: (i, 0), **({"pipeline_mode": once} if has_norm else {})),
        pl.BlockSpec((ms, k), lambda i, j: (0, 0)),
        pl.BlockSpec((None, k, tn), lambda i, j: (layer, 0, j)),
    ]
    args = [x, xs, w]
    scratch = []
    if has_norm:
        in_specs.append(pl.BlockSpec((1, k), lambda i, j: (0, 0)))
        args.append(norm_w.reshape(1, k))
        scratch = [pltpu.VMEM((tm, k), BF16), pltpu.VMEM((ms, k), BF16)]
    if bias is not None:
        in_specs.append(pl.BlockSpec((1, tn), lambda i, j: (0, j)))
        args.append(bias.reshape(1, n))
    if res is not None:
        in_specs.append(pl.BlockSpec((tm, tn), lambda i, j: (i, j)))
        in_specs.append(pl.BlockSpec((ms, tn), lambda i, j: (0, js(i, j))))
        args += [res, res_s]
    return pl.pallas_call(
        functools.partial(_matmul_kernel, has_norm=has_norm, has_bias=bias is not None, has_res=res is not None),
        out_shape=(jax.ShapeDtypeStruct((m, n), out_dtype), jax.ShapeDtypeStruct((ms, n), out_dtype)),
        grid=(m // tm, nj),
        in_specs=in_specs,
        out_specs=(pl.BlockSpec((tm, tn), lambda i, j: (i, j)), pl.BlockSpec((ms, tn), lambda i, j: (0, js(i, j)))),
        scratch_shapes=scratch,
        compiler_params=_params("arbitrary", "arbitrary"),
        name="matmul",
    )(*args)


FFN_ROWS = 1024
FFN_OUT_COLS = 512


def _ffn_step(xn_ref, wg, wu, wo, o_ref):
    xn = xn_ref[...]
    g = jnp.dot(xn, wg, preferred_element_type=F32)
    u = jnp.dot(xn, wu, preferred_element_type=F32)
    act = ((g * jax.nn.sigmoid(g)) * u).astype(BF16)
    d = o_ref.shape[1]
    for c in range(0, d, FFN_OUT_COLS):
        cs = slice(c, min(c + FFN_OUT_COLS, d))
        o_ref[:, cs] += jnp.dot(act, wo[:, cs], preferred_element_type=F32)


def _final_norm_in_place(o_ref, fw_ref):
    total = o_ref.shape[0]
    rows = min(NORM_ROWS, total)

    def body(r, carry):
        rs = pl.ds(pl.multiple_of(r * rows, rows), rows)
        o_ref[rs, :] = _rmsnorm_rows(o_ref[rs, :], fw_ref[...])
        return carry

    lax.fori_loop(0, total // rows, body, 0)


def _ffn_kernel(*refs, final_norm):
    h_ref, nw_ref = refs[0], refs[1]
    fw_ref = refs[2] if final_norm else None
    wg_ref, wu_ref, wo_ref, o_ref, xn_ref = refs[-5:]
    f = pl.program_id(1)

    @pl.when(f == 0)
    def _():
        _norm_rows_into(h_ref, nw_ref, xn_ref, copy_ref=o_ref)

    _ffn_step(xn_ref, wg_ref[...], wu_ref[...], wo_ref, o_ref)
    if final_norm:
        @pl.when(f == pl.num_programs(1) - 1)
        def _():
            _final_norm_in_place(o_ref, fw_ref)


def ffn(h, norm_w, wg, wu, wo, final_w=None):
    m, d = h.shape
    d_ff = wo.shape[0]
    tm = _tile(m, FFN_ROWS)
    tf = _tile(d_ff, 256)
    once = pl.Buffered(1)
    vec = pl.BlockSpec((1, d), lambda i, f: (0, 0), pipeline_mode=once)
    final_norm = final_w is not None
    return pl.pallas_call(
        functools.partial(_ffn_kernel, final_norm=final_norm),
        out_shape=jax.ShapeDtypeStruct((m, d), F32),
        grid=(m // tm, d_ff // tf),
        in_specs=[pl.BlockSpec((tm, d), lambda i, f: (i, 0), pipeline_mode=once), vec] + [vec] * final_norm + [
            pl.BlockSpec((d, tf), lambda i, f: (0, f)),
            pl.BlockSpec((d, tf), lambda i, f: (0, f)),
            pl.BlockSpec((tf, d), lambda i, f: (f, 0)),
        ],
        out_specs=pl.BlockSpec((tm, d), lambda i, f: (i, 0), pipeline_mode=once),
        scratch_shapes=[pltpu.VMEM((tm, d), BF16)],
        compiler_params=_params("parallel", "arbitrary"),
        name="ffn",
    )(h, norm_w.reshape(1, d), *([final_w.reshape(1, d)] if final_norm else []), wg, wu, wo)


def _ffn_sample_kernel(*refs, final_norm):
    h_ref, nw_ref = refs[0], refs[1]
    fw_ref = refs[2] if final_norm else None
    wg_ref, wu_ref, wo_ref, o_ref, wg16_ref, wu16_ref, wo16_ref, xn_ref = refs[-8:]
    f = pl.program_id(0)

    @pl.when(f == 0)
    def _():
        _norm_rows_into(h_ref, nw_ref, xn_ref, copy_ref=o_ref)

    wg16_ref[...] = wg_ref[...].astype(BF16)
    wu16_ref[...] = wu_ref[...].astype(BF16)
    wo16_ref[...] = wo_ref[...].astype(BF16)
    _ffn_step(xn_ref, wg16_ref[...], wu16_ref[...], wo16_ref, o_ref)
    if final_norm:
        @pl.when(f == pl.num_programs(0) - 1)
        def _():
            _final_norm_in_place(o_ref, fw_ref)


def ffn_sample(h, norm_w, w_in, w_out, layer, final_w=None):
    ms, d = h.shape
    d_ff = w_out.shape[1]
    tf = _tile(d_ff, 256)
    nf = d_ff // tf
    vec = pl.BlockSpec((1, d), lambda f: (0, 0))
    final_norm = final_w is not None
    return pl.pallas_call(
        functools.partial(_ffn_sample_kernel, final_norm=final_norm),
        out_shape=(
            jax.ShapeDtypeStruct((ms, d), F32),
            jax.ShapeDtypeStruct((d, d_ff), BF16),
            jax.ShapeDtypeStruct((d, d_ff), BF16),
            jax.ShapeDtypeStruct((d_ff, d), BF16),
        ),
        grid=(nf,),
        in_specs=[pl.BlockSpec((ms, d), lambda f: (0, 0)), vec] + [vec] * final_norm + [
            pl.BlockSpec((None, d, tf), lambda f: (layer, 0, f)),
            pl.BlockSpec((None, d, tf), lambda f: (layer, 0, f + nf)),
            pl.BlockSpec((None, tf, d), lambda f: (layer, f, 0)),
        ],
        out_specs=(
            pl.BlockSpec((ms, d), lambda f: (0, 0)),
            pl.BlockSpec((d, tf), lambda f: (0, f)),
            pl.BlockSpec((d, tf), lambda f: (0, f)),
            pl.BlockSpec((tf, d), lambda f: (f, 0)),
        ),
        scratch_shapes=[pltpu.VMEM((ms, d), BF16)],
        compiler_params=_params("arbitrary"),
        name="ffn_sample",
    )(h, norm_w.reshape(1, d), *([final_w.reshape(1, d)] if final_norm else []), w_in, w_in, w_out)


def _attn_prompt_kernel(sink_ref, q_ref, kc_ref, kp_ref, vc_ref, vp_ref, o_ref, *, n_kv, q_per_kv, hd, window):
    nb = pl.program_id(1)
    w = window
    assert 2 * hd == LANES and q_per_kv % 2 == 0 and n_kv % 2 == 0
    npair = q_per_kv // 2
    cols = npair * w
    q_low = lax.broadcasted_iota(jnp.int32, (w, LANES), 1) < hd
    kv_low = lax.broadcasted_iota(jnp.int32, (2 * w, LANES), 1) < hd
    kv_row0 = lax.broadcasted_iota(jnp.int32, (2 * w, LANES), 0) == 0
    kj = lax.broadcasted_iota(jnp.int32, (w, cols), 0)
    qi = lax.broadcasted_iota(jnp.int32, (w, cols), 1) & (w - 1)
    from_prev = kj > qi
    top_row0 = lax.broadcasted_iota(jnp.int32, (SUBLANES, cols), 0) == 0
    no_prev = jnp.where(nb > 0, 0.0, -jnp.inf)
    q_scale = hd ** -0.5 * LOG2E
    for h in range(n_kv):
        slab, half = divmod(h, 2)
        ksl = slice(slab * LANES, (slab + 1) * LANES)
        kk = [jnp.concatenate([kp_ref[:, ksl], kc_ref[:, ksl]], axis=0)]
        vv = [jnp.concatenate([vp_ref[:, ksl], vc_ref[:, ksl]], axis=0)]
        kk.append(pltpu.roll(kk[0], hd, axis=1))
        vv.append(pltpu.roll(vv[0], hd, axis=1))
        res = []
        for par in range(2):
            keep = q_low if par == 0 else ~q_low
            q = jnp.concatenate(
                [jnp.where(keep, q_ref[:, (h * npair + p) * LANES:(h * npair + p + 1) * LANES] * q_scale, 0.0)
                 for p in range(npair)], axis=0).astype(BF16)
            swap = int(par != half)
            s = lax.dot_general(kk[swap].astype(BF16), q, (((1,), (1,)), ((), ())), preferred_element_type=F32)
            s = jnp.where(from_prev, s[:w] + no_prev, s[w:])
            sink = jnp.concatenate(
                [jnp.full((1, w), sink_ref[h * q_per_kv + 2 * p + par] * LOG2E, F32) for p in range(npair)], axis=1)
            m = jnp.maximum(jnp.max(s, axis=0, keepdims=True), sink)
            e = jnp.exp2(s - m)
            e_prev = jnp.where(from_prev, e, 0.0)
            e_top = jnp.where(top_row0, jnp.exp2(sink - m), e_prev[:SUBLANES])
            e2 = jnp.concatenate([e_top, e_prev[SUBLANES:], jnp.where(from_prev, 0.0, e)], axis=0).astype(BF16)
            own = kv_low if par == 0 else ~kv_low
            v_aug = jnp.where(own, jnp.where(kv_row0, 0.0, vv[swap]), 1.0).astype(BF16)
            res.append(lax.dot_general(e2, v_aug, (((0,), (0,)), ((), ())), preferred_element_type=F32))
        for p in range(npair):
            r0, r1 = res[0][p * w:(p + 1) * w], res[1][p * w:(p + 1) * w]
            num = jnp.where(q_low, r0, r1)
            den = pltpu.roll(jnp.where(q_low, r1, r0), hd, axis=1)
            c = (h * npair + p) * LANES
            o_ref[:, c:c + LANES] = (num * (1.0 / den)).astype(o_ref.dtype)


def attn_prompt(qkv, sinks, batch, seq, n_kv, q_per_kv, hd, window):
    q_dim = n_kv * q_per_kv * hd
    kv_dim = n_kv * hd
    nblk = seq // window
    kcol = q_dim // kv_dim
    cur = lambda b, n: b * nblk + n
    prev = lambda b, n: b * nblk + jnp.maximum(n - 1, 0)
    return pl.pallas_call(
        functools.partial(_attn_prompt_kernel, n_kv=n_kv, q_per_kv=q_per_kv, hd=hd, window=window),
        out_shape=jax.ShapeDtypeStruct((batch * seq, q_dim), BF16),
        grid=(batch, nblk),
        in_specs=[
            pl.BlockSpec(memory_space=pltpu.SMEM),
            pl.BlockSpec((window, q_dim), lambda b, n: (cur(b, n), 0)),
            pl.BlockSpec((window, kv_dim), lambda b, n: (cur(b, n), kcol)),
            pl.BlockSpec((window, kv_dim), lambda b, n: (prev(b, n), kcol)),
            pl.BlockSpec((window, kv_dim), lambda b, n: (cur(b, n), kcol + 1)),
            pl.BlockSpec((window, kv_dim), lambda b, n: (prev(b, n), kcol + 1)),
        ],
        out_specs=pl.BlockSpec((window, q_dim), lambda b, n: (cur(b, n), 0)),
        compiler_params=_params("parallel", "arbitrary"),
        name="attn_prompt",
    )(sinks, qkv, qkv, qkv, qkv, qkv)


def _attn_sample_kernel(sink_ref, q_ref, kn_ref, vn_ref, ck_ref, cv_ref, o_ref, wk_ref, wv_ref, *, n_kv, q_per_kv, hd,
                        window):
    w = window
    wk_ref[0, 0:w - 1, :] = ck_ref[0, 1:w, :]
    wk_ref[0, w - 1:w, :] = kn_ref[0]
    wv_ref[0, 0:w - 1, :] = cv_ref[0, 1:w, :]
    wv_ref[0, w - 1:w, :] = vn_ref[0]
    scale = hd ** -0.5
    for h in range(n_kv):
        rs = slice(h * q_per_kv, (h + 1) * q_per_kv)
        ks = slice(h * hd, (h + 1) * hd)
        q = (q_ref[0, rs, :] * scale).astype(BF16)
        kk = wk_ref[0, :, ks].astype(BF16)
        vv = wv_ref[0, :, ks].astype(BF16)
        s = lax.dot_general(q, kk, (((1,), (1,)), ((), ())), preferred_element_type=F32)
        sink = sink_ref[rs, :]
        m = jnp.maximum(jnp.max(s, axis=-1, keepdims=True), sink)
        e = jnp.exp(s - m)
        den = jnp.sum(e, axis=-1, keepdims=True) + jnp.exp(sink - m)
        o_ref[0, rs, :] = jnp.dot(e.astype(BF16), vv, preferred_element_type=F32) * (1.0 / den)


def attn_sample(q, k_new, v_new, cache_k, cache_v, sinks, n_kv, q_per_kv, hd, window):
    nb = q.shape[0]
    nq = n_kv * q_per_kv
    kv_dim = n_kv * hd
    win = pl.BlockSpec((1, window, kv_dim), lambda i: (i, 0, 0))
    row = pl.BlockSpec((1, 1, kv_dim), lambda i: (i, 0, 0))
    qs = pl.BlockSpec((1, nq, hd), lambda i: (i, 0, 0))
    return pl.pallas_call(
        functools.partial(_attn_sample_kernel, n_kv=n_kv, q_per_kv=q_per_kv, hd=hd, window=window),
        out_shape=(
            jax.ShapeDtypeStruct((nb, nq, hd), F32),
            jax.ShapeDtypeStruct((nb, window, kv_dim), F32),
            jax.ShapeDtypeStruct((nb, window, kv_dim), F32),
        ),
        grid=(nb,),
        in_specs=[pl.BlockSpec((nq, 1), lambda i: (0, 0)), qs, row, row, win, win],
        out_specs=(qs, win, win),
        compiler_params=_params("parallel"),
        name="attn_sample",
    )(sinks.reshape(nq, 1), q, k_new, v_new, cache_k, cache_v)


def _lower_bound(l, layer):
    e = jnp.exp(l - jnp.max(l, axis=0, keepdims=True))
    p = e / jnp.sum(e, axis=0, keepdims=True)
    return jnp.sum(p[0:layer + 1], axis=0, keepdims=True) - p[0:1]


def _cumsum_rows(x, tri):
    k = x.shape[1]
    hi = x.astype(BF16)
    r1 = x - hi.astype(F32)
    mid = r1.astype(BF16)
    lo = (r1 - mid.astype(F32)).astype(BF16)
    parts = jnp.dot(tri, jnp.concatenate([hi, mid, lo], axis=1), preferred_element_type=F32)
    return (parts[:, 0:k] + parts[:, k:2 * k]) + parts[:, 2 * k:3 * k]


def _block_ref_rows(b, half):
    c, k = b.shape
    blk = 2 * half
    if blk >= 8:
        b3 = b.reshape(c // blk, blk, k)
        return jnp.broadcast_to(b3[:, half - 1:half, :], (c // blk, blk, k)).reshape(c, k)
    b3 = b.reshape(c // 8, 8, k)
    sub = lax.broadcasted_iota(jnp.int32, (c // 8, 8, k), 1)
    out = None
    for j in range(8 // blk):
        r = j * blk + half - 1
        cand = jnp.broadcast_to(b3[:, r:r + 1, :], (c // 8, 8, k))
        out = cand if out is None else jnp.where(sub >= j * blk, cand, out)
    return out.reshape(c, k)


def _gla_prompt_kernel(lb_ref, nw_ref, q_ref, f_ref, i_ref, g_ref, og_ref, st_ref, s_scr, *, layer, n_heads, kd, vd):
    c = q_ref.shape[0]
    ci = pl.program_id(1)
    n_chunks = pl.num_programs(1)

    @pl.when(ci == 0)
    def _():
        s_scr[...] = jnp.zeros_like(s_scr)

    nw = nw_ref[...]
    ti = lax.broadcasted_iota(jnp.int32, (c, c), 0)
    si = lax.broadcasted_iota(jnp.int32, (c, c), 1)
    pair_bits = jnp.where(ti > si, ti ^ si, 0)
    tri = (ti >= si).astype(BF16)
    lane = lax.broadcasted_iota(jnp.int32, (SUBLANES, c), 1)
    q_scale = kd ** -0.5
    zero_rows = {hf: jnp.zeros((hf, kd), F32) for hf in (SUBLANES << j for j in range(8)) if hf < c}

    def head(h, carry):
        ks = pl.ds(pl.multiple_of(h * kd, kd), kd)
        vs = pl.ds(pl.multiple_of(h * vd, vd), vd)
        lb = _lower_bound(lb_ref[:, ks], layer)
        qr = q_ref[:, ks]
        q = (qr * jax.nn.sigmoid(qr)) * q_scale
        forget = lb + (1.0 - lb) * jax.nn.sigmoid(f_ref[:, ks])
        k = 1.0 - forget
        v = i_ref[:, vs]
        b2 = _cumsum_rows(jnp.log2(forget), tri)
        b_last = b2[c - 1:c, :]
        v16 = v.astype(BF16)
        s_t = s_scr[h]
        o = lax.dot_general((q * jnp.exp2(b2)).astype(BF16), s_t.astype(BF16), (((1,), (1,)), ((), ())),
                            preferred_element_type=F32)
        kdec = (k * jnp.exp2(b_last - b2)).astype(BF16)
        s_new = jnp.exp2(b_last) * s_t + lax.dot_general(v16, kdec, (((0,), (0,)), ((), ())),
                                                        preferred_element_type=F32)
        s_scr[h] = s_new

        a = jnp.zeros((c, c), F32)
        hf = 1
        while hf < min(SUBLANES, c):
            e = jnp.exp2(-jnp.abs(b2 - _block_ref_rows(b2, hf)))
            al = lax.dot_general((q * e).astype(BF16), (k * e).astype(BF16), (((1,), (1,)), ((), ())),
                                 preferred_element_type=F32)
            a = jnp.where(pair_bits >= hf, al, a)
            hf *= 2
        a_rows = [a[r:r + SUBLANES] for r in range(0, c, SUBLANES)]

        while hf < c:
            q_parts, k_parts, uppers = [], [], []
            for blk in range(0, c, 2 * hf):
                lo, up = slice(blk, blk + hf), slice(blk + hf, blk + 2 * hf)
                r = b2[blk + hf - 1:blk + hf, :]
                q_parts.append(q[up] * jnp.exp2(b2[up] - r))
                k_parts += [k[lo] * jnp.exp2(r - b2[lo]), zero_rows[hf]]
                uppers += [(p, blk) for p in range((blk + hf) // SUBLANES, (blk + 2 * hf) // SUBLANES)]
            al = lax.dot_general(jnp.concatenate(q_parts, axis=0).astype(BF16),
                                 jnp.concatenate(k_parts, axis=0).astype(BF16),
                                 (((1,), (1,)), ((), ())), preferred_element_type=F32)
            for j, (p, blk) in enumerate(uppers):
                same_block = (lane >= blk) & (lane < blk + hf)
                a_rows[p] = jnp.where(same_block, al[j * SUBLANES:(j + 1) * SUBLANES], a_rows[p])
            hf *= 2

        o = o + jnp.dot(jnp.concatenate(a_rows, axis=0).astype(BF16), v16, preferred_element_type=F32)
        o = o + jnp.sum(q * k, axis=-1, keepdims=True) * v
        o = (o * lax.rsqrt(jnp.mean(o * o, axis=-1, keepdims=True) + RMS_EPS)) * nw
        gr = g_ref[:, vs]
        og_ref[:, vs] = (o * (gr * jax.nn.sigmoid(gr))).astype(og_ref.dtype)
        return carry

    lax.fori_loop(0, n_heads, head, 0, unroll=4)

    @pl.when(ci == n_chunks - 1)
    def _():
        def emit(h, carry):
            st_ref[0, h] = s_scr[h].T
            return carry

        lax.fori_loop(0, n_heads, emit, 0)


def gla_prompt(u, lower_bounds, norm_w, layer, batch, seq, n_heads, kd, vd):
    d = n_heads * kd
    c = _tile(seq, GLA_CHUNK)
    nch = seq // c
    n_layers = lower_bounds.shape[0]
    col = lambda j: pl.BlockSpec((c, d), lambda b, i: (b * nch + i, j))
    return pl.pallas_call(
        functools.partial(_gla_prompt_kernel, layer=layer, n_heads=n_heads, kd=kd, vd=vd),
        out_shape=(
            jax.ShapeDtypeStruct((batch * seq, d), BF16),
            jax.ShapeDtypeStruct((batch, n_heads, kd, vd), F32),
        ),
        grid=(batch, nch),
        in_specs=[
            pl.BlockSpec((n_layers, d), lambda b, i: (0, 0)),
            pl.BlockSpec((1, vd), lambda b, i: (0, 0)),
            col(0), col(1), col(2), col(3),
        ],
        out_specs=(
            pl.BlockSpec((c, d), lambda b, i: (b * nch + i, 0)),
            pl.BlockSpec((1, n_heads, kd, vd), lambda b, i: (b, 0, 0, 0)),
        ),
        scratch_shapes=[pltpu.VMEM((n_heads, vd, kd), F32)],
        compiler_params=_params("parallel", "arbitrary"),
        name="gla_prompt",
    )(lower_bounds, norm_w.reshape(1, vd), u, u, u, u)


def _gla_sample_kernel(lb_ref, nw_ref, x_ref, s_ref, og_ref, so_ref, *, layer, n_heads, kd):
    nh = n_heads
    x = x_ref[0]
    lb = _lower_bound(lb_ref[...], layer)[0]
    qr = x[0:nh]
    q = (qr * jax.nn.sigmoid(qr)) * (kd ** -0.5)
    forget = lb + (1.0 - lb) * jax.nn.sigmoid(x[nh:2 * nh])
    k = 1.0 - forget
    v = x[2 * nh:3 * nh]
    gr = x[3 * nh:4 * nh]
    pad = jnp.zeros((x.shape[0] - 3 * nh, kd), F32)
    cols = jnp.concatenate([q, forget, k, pad], axis=0).T
    outs = []
    for h in range(nh):
        s_new = cols[:, nh + h:nh + h + 1] * s_ref[0, h] + cols[:, 2 * nh + h:2 * nh + h + 1] * v[h:h + 1, :]
        so_ref[0, h] = s_new
        outs.append(jnp.sum(cols[:, h:h + 1] * s_new, axis=0, keepdims=True))
    o = jnp.concatenate(outs, axis=0)
    o = (o * lax.rsqrt(jnp.mean(o * o, axis=-1, keepdims=True) + RMS_EPS)) * nw_ref[...]
    og_ref[0] = o * (gr * jax.nn.sigmoid(gr))


def gla_sample(u, state, lower_bounds, norm_w, layer, n_heads, kd, vd):
    nb = u.shape[0]
    n_layers = lower_bounds.shape[0]
    rows = 4 * n_heads
    return pl.pallas_call(
        functools.partial(_gla_sample_kernel, layer=layer, n_heads=n_heads, kd=kd),
        out_shape=(
            jax.ShapeDtypeStruct((nb, n_heads, vd), F32),
            jax.ShapeDtypeStruct((nb, n_heads, kd, vd), F32),
        ),
        grid=(nb,),
        in_specs=[
            pl.BlockSpec((n_layers, n_heads, kd), lambda i: (0, 0, 0)),
            pl.BlockSpec((1, vd), lambda i: (0, 0)),
            pl.BlockSpec((1, rows, kd), lambda i: (i, 0, 0)),
            pl.BlockSpec((1, n_heads, kd, vd), lambda i: (i, 0, 0, 0)),
        ],
        out_specs=(
            pl.BlockSpec((1, n_heads, vd), lambda i: (i, 0, 0)),
            pl.BlockSpec((1, n_heads, kd, vd), lambda i: (i, 0, 0, 0)),
        ),
        compiler_params=_params("parallel"),
        name="gla_sample",
    )(lower_bounds.reshape(n_layers, n_heads, kd), norm_w.reshape(1, vd), u.reshape(nb, rows, kd), state)


def kernel(x_prompt, x_sample, cache_k, cache_v, state_hgrn, norm_mix, norm_ffn, norm_final, attn_w_qkv, attn_b_qkv,
           attn_sinks, attn_w_o, attn_b_o, hgrn_w_in, hgrn_lower_bounds, hgrn_norm, hgrn_w_o, ffn_w_in, ffn_w_out):
    batch, seq, d = x_prompt.shape
    nb = x_sample.shape[0]
    assert x_sample.shape[1] == 1
    depth = norm_mix.shape[0]
    window, n_kv, hd = cache_k.shape[2:]
    n_q = attn_sinks.shape[1]
    q_per_kv = n_q // n_kv
    q_dim, kv_dim = n_q * hd, n_kv * hd
    n_heads, kd, vd = state_hgrn.shape[2:]

    hp = x_prompt.reshape(batch * seq, d)
    hs = x_sample.reshape(nb, d)
    wk_p, wv_p, st_p, wk_s, wv_s, st_s = [], [], [], [], [], []
    for i in range(depth):
        if i % 2 == 0:
            a = i // 2
            qkv_p, qkv_s = matmul(hp, hs, attn_w_qkv, a, norm_w=norm_mix[i], bias=attn_b_qkv[a])
            op = attn_prompt(qkv_p, attn_sinks[a], batch, seq, n_kv, q_per_kv, hd, window)
            kv_p = qkv_p.reshape(batch, seq, q_dim + 2 * kv_dim)[:, seq - window:, q_dim:]
            wk_p.append(kv_p[..., :kv_dim].reshape(batch, window, n_kv, hd))
            wv_p.append(kv_p[..., kv_dim:].reshape(batch, window, n_kv, hd))
            os_, wk, wv = attn_sample(
                qkv_s[:, :q_dim].reshape(nb, n_q, hd),
                qkv_s[:, q_dim:q_dim + kv_dim].reshape(nb, 1, kv_dim),
                qkv_s[:, q_dim + kv_dim:].reshape(nb, 1, kv_dim),
                cache_k[a].reshape(nb, window, kv_dim), cache_v[a].reshape(nb, window, kv_dim),
                attn_sinks[a], n_kv, q_per_kv, hd, window)
            wk_s.append(wk.reshape(nb, window, n_kv, hd))
            wv_s.append(wv.reshape(nb, window, n_kv, hd))
            hp, hs = matmul(op, os_.reshape(nb, q_dim).astype(BF16), attn_w_o, a, bias=attn_b_o[a], res=hp, res_s=hs)
        else:
            r = i // 2
            up, us = matmul(hp, hs, hgrn_w_in, r, norm_w=norm_mix[i])
            ogp, sp = gla_prompt(up, hgrn_lower_bounds, hgrn_norm[r], i, batch, seq, n_heads, kd, vd)
            ogs, ss = gla_sample(us, state_hgrn[r], hgrn_lower_bounds, hgrn_norm[r], i, n_heads, kd, vd)
            st_p.append(sp)
            st_s.append(ss)
            hp, hs = matmul(ogp, ogs.reshape(nb, d).astype(BF16), hgrn_w_o, r, res=hp, res_s=hs)
        final_w = norm_final if i == depth - 1 else None
        hs, wg, wu, wo = ffn_sample(hs, norm_ffn[i], ffn_w_in, ffn_w_out, i, final_w=final_w)
        hp = ffn(hp, norm_ffn[i], wg, wu, wo, final_w=final_w)

    y_prompt = hp.reshape(batch, seq, d)
    y_sample = hs.reshape(nb, 1, d)
    return (y_prompt, y_sample, jnp.stack(wk_p), jnp.stack(wv_p), jnp.stack(st_p), jnp.stack(wk_s), jnp.stack(wv_s),
            jnp.stack(st_s))
```

```python
import functools

import jax
import jax.numpy as jnp
from jax import lax
from jax.experimental import pallas as pl
from jax.experimental.pallas import tpu as pltpu

RMS_EPS = 1e-5
BF16 = jnp.bfloat16
F32 = jnp.float32

V7X_VMEM_LIMIT_BYTES = 56 * 1024 * 1024
GLA_CHUNK = 128
GLA_HEAD_GROUP = 8
SUBLANES = 8
LANES = 128
LOG2E = 1.4426950408889634


def _params(*sem):
    return pltpu.CompilerParams(dimension_semantics=sem, vmem_limit_bytes=V7X_VMEM_LIMIT_BYTES)


def _tile(n, pref):
    return pref if n % pref == 0 else n


def _rmsnorm_rows(x, w):
    ms = jnp.mean(x * x, axis=-1, keepdims=True)
    return (x * lax.rsqrt(ms + RMS_EPS)) * w


NORM_ROWS = 16
NORM_UNROLL = 4


def _norm_rows_into(src_ref, nw_ref, dst_ref, copy_ref=None):
    total = src_ref.shape[0]
    rows = min(NORM_ROWS, total)

    def body(r, carry):
        rs = pl.ds(pl.multiple_of(r * rows, rows), rows)
        h = src_ref[rs, :]
        dst_ref[rs, :] = _rmsnorm_rows(h, nw_ref[...]).astype(BF16)
        if copy_ref is not None:
            copy_ref[rs, :] = h
        return carry

    lax.fori_loop(0, total // rows, body, 0, unroll=min(NORM_UNROLL, total // rows))


def _matmul_kernel(*refs, has_norm, has_bias, has_res):
    x_ref, xs_ref, w_ref = refs[0], refs[1], refs[2]
    pos = 3
    nw_ref = b_ref = r_ref = rs_ref = None
    if has_norm:
        nw_ref = refs[pos]
        pos += 1
    if has_bias:
        b_ref = refs[pos]
        pos += 1
    if has_res:
        r_ref, rs_ref = refs[pos], refs[pos + 1]
        pos += 2
    o_ref, os_ref = refs[pos], refs[pos + 1]
    i, j = pl.program_id(0), pl.program_id(1)
    if has_norm:
        xn_ref, xsn_ref = refs[pos + 2], refs[pos + 3]

        @pl.when(j == 0)
        def _():
            _norm_rows_into(x_ref, nw_ref, xn_ref)

        @pl.when((j == 0) & (i == 0))
        def _():
            _norm_rows_into(xs_ref, nw_ref, xsn_ref)
    else:
        xn_ref, xsn_ref = x_ref, xs_ref
    w = w_ref[...].astype(BF16)

    def finish(acc, res_ref, out_ref):
        if has_bias:
            acc = acc + b_ref[...]
        if has_res:
            acc = acc + res_ref[...]
        out_ref[...] = acc.astype(out_ref.dtype)

    finish(jnp.dot(xn_ref[...], w, preferred_element_type=F32), r_ref, o_ref)

    @pl.when(i == 0)
    def _():
        finish(jnp.dot(xsn_ref[...], w, preferred_element_type=F32), rs_ref, os_ref)


def matmul(x, xs, w, layer, norm_w=None, bias=None, res=None, res_s=None, out_dtype=F32):
    m, k = x.shape
    ms = xs.shape[0]
    n = w.shape[2]
    tm = _tile(m, 1024)
    tn = _tile(n, 512)
    nj = n // tn
    js = lambda i, j: jnp.where(i == 0, j, nj - 1)
    has_norm = norm_w is not None
    once = pl.Buffered(1)
    in_specs = [
        pl.BlockSpec((tm, k), lambda i, j: (i, 0), **({"pipeline_mode": once} if has_norm else {})),
        pl.BlockSpec((ms, k), lambda i, j: (0, 0)),
        pl.BlockSpec((None, k, tn), lambda i, j: (layer, 0, j)),
    ]
    args = [x, xs, w]
    scratch = []
    if has_norm:
        in_specs.append(pl.BlockSpec((1, k), lambda i, j: (0, 0)))
        args.append(norm_w.reshape(1, k))
        scratch = [pltpu.VMEM((tm, k), BF16), pltpu.VMEM((ms, k), BF16)]
    if bias is not None:
        in_specs.append(pl.BlockSpec((1, tn), lambda i, j: (0, j)))
        args.append(bias.reshape(1, n))
    if res is not None:
        in_specs.append(pl.BlockSpec((tm, tn), lambda i, j: (i, j)))
        in_specs.append(pl.BlockSpec((ms, tn), lambda i, j: (0, js(i, j))))
        args += [res, res_s]
    return pl.pallas_call(
        functools.partial(_matmul_kernel, has_norm=has_norm, has_bias=bias is not None, has_res=res is not None),
        out_shape=(jax.ShapeDtypeStruct((m, n), out_dtype), jax.ShapeDtypeStruct((ms, n), out_dtype)),
        grid=(m // tm, nj),
        in_specs=in_specs,
        out_specs=(pl.BlockSpec((tm, tn), lambda i, j: (i, j)), pl.BlockSpec((ms, tn), lambda i, j: (0, js(i, j)))),
        scratch_shapes=scratch,
        compiler_params=_params("arbitrary", "arbitrary"),
        name="matmul",
    )(*args)


FFN_ROWS = 1024
FFN_OUT_COLS = 512


def _ffn_step(xn_ref, wg, wu, wo, o_ref):
    xn = xn_ref[...]
    g = jnp.dot(xn, wg, preferred_element_type=F32)
    u = jnp.dot(xn, wu, preferred_element_type=F32)
    act = ((g * jax.nn.sigmoid(g)) * u).astype(BF16)
    d = o_ref.shape[1]
    for c in range(0, d, FFN_OUT_COLS):
        cs = slice(c, min(c + FFN_OUT_COLS, d))
        o_ref[:, cs] += jnp.dot(act, wo[:, cs], preferred_element_type=F32)


def _final_norm_in_place(o_ref, fw_ref):
    total = o_ref.shape[0]
    rows = min(NORM_ROWS, total)

    def body(r, carry):
        rs = pl.ds(pl.multiple_of(r * rows, rows), rows)
        o_ref[rs, :] = _rmsnorm_rows(o_ref[rs, :], fw_ref[...])
        return carry

    lax.fori_loop(0, total // rows, body, 0, unroll=min(NORM_UNROLL, total // rows))


def _ffn_kernel(*refs, final_norm):
    h_ref, nw_ref = refs[0], refs[1]
    fw_ref = refs[2] if final_norm else None
    wg_ref, wu_ref, wo_ref, o_ref, xn_ref = refs[-5:]
    f = pl.program_id(1)

    @pl.when(f == 0)
    def _():
        _norm_rows_into(h_ref, nw_ref, xn_ref, copy_ref=o_ref)

    _ffn_step(xn_ref, wg_ref[...], wu_ref[...], wo_ref, o_ref)
    if final_norm:
        @pl.when(f == pl.num_programs(1) - 1)
        def _():
            _final_norm_in_place(o_ref, fw_ref)


def ffn(h, norm_w, wg, wu, wo, final_w=None):
    m, d = h.shape
    d_ff = wo.shape[0]
    tm = _tile(m, FFN_ROWS)
    tf = _tile(d_ff, 256)
    once = pl.Buffered(1)
    vec = pl.BlockSpec((1, d), lambda i, f: (0, 0), pipeline_mode=once)
    final_norm = final_w is not None
    return pl.pallas_call(
        functools.partial(_ffn_kernel, final_norm=final_norm),
        out_shape=jax.ShapeDtypeStruct((m, d), F32),
        grid=(m // tm, d_ff // tf),
        in_specs=[pl.BlockSpec((tm, d), lambda i, f: (i, 0), pipeline_mode=once), vec] + [vec] * final_norm + [
            pl.BlockSpec((d, tf), lambda i, f: (0, f)),
            pl.BlockSpec((d, tf), lambda i, f: (0, f)),
            pl.BlockSpec((tf, d), lambda i, f: (f, 0)),
        ],
        out_specs=pl.BlockSpec((tm, d), lambda i, f: (i, 0), pipeline_mode=once),
        scratch_shapes=[pltpu.VMEM((tm, d), BF16)],
        compiler_params=_params("parallel", "arbitrary"),
        name="ffn",
    )(h, norm_w.reshape(1, d), *([final_w.reshape(1, d)] if final_norm else []), wg, wu, wo)


def _ffn_sample_kernel(*refs, final_norm):
    h_ref, nw_ref = refs[0], refs[1]
    fw_ref = refs[2] if final_norm else None
    wg_ref, wu_ref, wo_ref, o_ref, wg16_ref, wu16_ref, wo16_ref, xn_ref = refs[-8:]
    f = pl.program_id(0)

    @pl.when(f == 0)
    def _():
        _norm_rows_into(h_ref, nw_ref, xn_ref, copy_ref=o_ref)

    wg16_ref[...] = wg_ref[...].astype(BF16)
    wu16_ref[...] = wu_ref[...].astype(BF16)
    wo16_ref[...] = wo_ref[...].astype(BF16)
    _ffn_step(xn_ref, wg16_ref[...], wu16_ref[...], wo16_ref, o_ref)
    if final_norm:
        @pl.when(f == pl.num_programs(0) - 1)
        def _():
            _final_norm_in_place(o_ref, fw_ref)


def ffn_sample(h, norm_w, w_in, w_out, layer, final_w=None):
    ms, d = h.shape
    d_ff = w_out.shape[1]
    tf = _tile(d_ff, 256)
    nf = d_ff // tf
    vec = pl.BlockSpec((1, d), lambda f: (0, 0))
    final_norm = final_w is not None
    return pl.pallas_call(
        functools.partial(_ffn_sample_kernel, final_norm=final_norm),
        out_shape=(
            jax.ShapeDtypeStruct((ms, d), F32),
            jax.ShapeDtypeStruct((d, d_ff), BF16),
            jax.ShapeDtypeStruct((d, d_ff), BF16),
            jax.ShapeDtypeStruct((d_ff, d), BF16),
        ),
        grid=(nf,),
        in_specs=[pl.BlockSpec((ms, d), lambda f: (0, 0)), vec] + [vec] * final_norm + [
            pl.BlockSpec((None, d, tf), lambda f: (layer, 0, f)),
            pl.BlockSpec((None, d, tf), lambda f: (layer, 0, f + nf)),
            pl.BlockSpec((None, tf, d), lambda f: (layer, f, 0)),
        ],
        out_specs=(
            pl.BlockSpec((ms, d), lambda f: (0, 0)),
            pl.BlockSpec((d, tf), lambda f: (0, f)),
            pl.BlockSpec((d, tf), lambda f: (0, f)),
            pl.BlockSpec((tf, d), lambda f: (f, 0)),
        ),
        scratch_shapes=[pltpu.VMEM((ms, d), BF16)],
        compiler_params=_params("arbitrary"),
        name="ffn_sample",
    )(h, norm_w.reshape(1, d), *([final_w.reshape(1, d)] if final_norm else []), w_in, w_in, w_out)


def _attn_prompt_kernel(sink_ref, q_ref, kc_ref, kp_ref, vc_ref, vp_ref, o_ref, *, n_kv, q_per_kv, hd, window):
    nb = pl.program_id(1)
    w = window
    assert 2 * hd == LANES and q_per_kv % 2 == 0 and n_kv % 2 == 0
    npair = q_per_kv // 2
    cols = npair * w
    q_low = lax.broadcasted_iota(jnp.int32, (w, LANES), 1) < hd
    kv_low = lax.broadcasted_iota(jnp.int32, (2 * w, LANES), 1) < hd
    kv_row0 = lax.broadcasted_iota(jnp.int32, (2 * w, LANES), 0) == 0
    kj = lax.broadcasted_iota(jnp.int32, (w, cols), 0)
    qi = lax.broadcasted_iota(jnp.int32, (w, cols), 1) & (w - 1)
    from_prev = kj > qi
    top_row0 = lax.broadcasted_iota(jnp.int32, (SUBLANES, cols), 0) == 0
    no_prev = jnp.where(nb > 0, 0.0, -jnp.inf)
    q_scale = hd ** -0.5 * LOG2E
    for h in range(n_kv):
        slab, half = divmod(h, 2)
        ksl = slice(slab * LANES, (slab + 1) * LANES)
        kk = [jnp.concatenate([kp_ref[:, ksl], kc_ref[:, ksl]], axis=0)]
        vv = [jnp.concatenate([vp_ref[:, ksl], vc_ref[:, ksl]], axis=0)]
        kk.append(pltpu.roll(kk[0], hd, axis=1))
        vv.append(pltpu.roll(vv[0], hd, axis=1))
        res = []
        for par in range(2):
            keep = q_low if par == 0 else ~q_low
            q = jnp.concatenate(
                [jnp.where(keep, q_ref[:, (h * npair + p) * LANES:(h * npair + p + 1) * LANES] * q_scale, 0.0)
                 for p in range(npair)], axis=0).astype(BF16)
            swap = int(par != half)
            s = lax.dot_general(kk[swap].astype(BF16), q, (((1,), (1,)), ((), ())), preferred_element_type=F32)
            s = jnp.where(from_prev, s[:w] + no_prev, s[w:])
            sink = jnp.concatenate(
                [jnp.full((1, w), sink_ref[h * q_per_kv + 2 * p + par] * LOG2E, F32) for p in range(npair)], axis=1)
            m = jnp.maximum(jnp.max(s, axis=0, keepdims=True), sink)
            e = jnp.exp2(s - m)
            e_prev = jnp.where(from_prev, e, 0.0)
            e_top = jnp.where(top_row0, jnp.exp2(sink - m), e_prev[:SUBLANES])
            e2 = jnp.concatenate([e_top, e_prev[SUBLANES:], jnp.where(from_prev, 0.0, e)], axis=0).astype(BF16)
            own = kv_low if par == 0 else ~kv_low
            v_aug = jnp.where(own, jnp.where(kv_row0, 0.0, vv[swap]), 1.0).astype(BF16)
            res.append(lax.dot_general(e2, v_aug, (((0,), (0,)), ((), ())), preferred_element_type=F32))
        for p in range(npair):
            r0, r1 = res[0][p * w:(p + 1) * w], res[1][p * w:(p + 1) * w]
            num = jnp.where(q_low, r0, r1)
            den = pltpu.roll(jnp.where(q_low, r1, r0), hd, axis=1)
            c = (h * npair + p) * LANES
            o_ref[:, c:c + LANES] = (num * (1.0 / den)).astype(o_ref.dtype)


def attn_prompt(qkv, sinks, batch, seq, n_kv, q_per_kv, hd, window):
    q_dim = n_kv * q_per_kv * hd
    kv_dim = n_kv * hd
    nblk = seq // window
    kcol = q_dim // kv_dim
    cur = lambda b, n: b * nblk + n
    prev = lambda b, n: b * nblk + jnp.maximum(n - 1, 0)
    return pl.pallas_call(
        functools.partial(_attn_prompt_kernel, n_kv=n_kv, q_per_kv=q_per_kv, hd=hd, window=window),
        out_shape=jax.ShapeDtypeStruct((batch * seq, q_dim), BF16),
        grid=(batch, nblk),
        in_specs=[
            pl.BlockSpec(memory_space=pltpu.SMEM),
            pl.BlockSpec((window, q_dim), lambda b, n: (cur(b, n), 0)),
            pl.BlockSpec((window, kv_dim), lambda b, n: (cur(b, n), kcol)),
            pl.BlockSpec((window, kv_dim), lambda b, n: (prev(b, n), kcol)),
            pl.BlockSpec((window, kv_dim), lambda b, n: (cur(b, n), kcol + 1)),
            pl.BlockSpec((window, kv_dim), lambda b, n: (prev(b, n), kcol + 1)),
        ],
        out_specs=pl.BlockSpec((window, q_dim), lambda b, n: (cur(b, n), 0)),
        compiler_params=_params("parallel", "arbitrary"),
        name="attn_prompt",
    )(sinks, qkv, qkv, qkv, qkv, qkv)


def _attn_sample_kernel(sink_ref, q_ref, kn_ref, vn_ref, ck_ref, cv_ref, o_ref, wk_ref, wv_ref, *, n_kv, q_per_kv, hd,
                        window):
    w = window
    wk_ref[0, 0:w - 1, :] = ck_ref[0, 1:w, :]
    wk_ref[0, w - 1:w, :] = kn_ref[0]
    wv_ref[0, 0:w - 1, :] = cv_ref[0, 1:w, :]
    wv_ref[0, w - 1:w, :] = vn_ref[0]
    scale = hd ** -0.5
    for h in range(n_kv):
        rs = slice(h * q_per_kv, (h + 1) * q_per_kv)
        ks = slice(h * hd, (h + 1) * hd)
        q = (q_ref[0, rs, :] * scale).astype(BF16)
        kk = wk_ref[0, :, ks].astype(BF16)
        vv = wv_ref[0, :, ks].astype(BF16)
        s = lax.dot_general(q, kk, (((1,), (1,)), ((), ())), preferred_element_type=F32)
        sink = sink_ref[rs, :]
        m = jnp.maximum(jnp.max(s, axis=-1, keepdims=True), sink)
        e = jnp.exp(s - m)
        den = jnp.sum(e, axis=-1, keepdims=True) + jnp.exp(sink - m)
        o_ref[0, rs, :] = jnp.dot(e.astype(BF16), vv, preferred_element_type=F32) * (1.0 / den)


def attn_sample(q, k_new, v_new, cache_k, cache_v, sinks, n_kv, q_per_kv, hd, window):
    nb = q.shape[0]
    nq = n_kv * q_per_kv
    kv_dim = n_kv * hd
    win = pl.BlockSpec((1, window, kv_dim), lambda i: (i, 0, 0))
    row = pl.BlockSpec((1, 1, kv_dim), lambda i: (i, 0, 0))
    qs = pl.BlockSpec((1, nq, hd), lambda i: (i, 0, 0))
    return pl.pallas_call(
        functools.partial(_attn_sample_kernel, n_kv=n_kv, q_per_kv=q_per_kv, hd=hd, window=window),
        out_shape=(
            jax.ShapeDtypeStruct((nb, nq, hd), F32),
            jax.ShapeDtypeStruct((nb, window, kv_dim), F32),
            jax.ShapeDtypeStruct((nb, window, kv_dim), F32),
        ),
        grid=(nb,),
        in_specs=[pl.BlockSpec((nq, 1), lambda i: (0, 0)), qs, row, row, win, win],
        out_specs=(qs, win, win),
        compiler_params=_params("parallel"),
        name="attn_sample",
    )(sinks.reshape(nq, 1), q, k_new, v_new, cache_k, cache_v)


def _lower_bound(l, layer):
    e = jnp.exp(l - jnp.max(l, axis=0, keepdims=True))
    p = e / jnp.sum(e, axis=0, keepdims=True)
    return jnp.sum(p[0:layer + 1], axis=0, keepdims=True) - p[0:1]


def _cumsum_rows(x, tri):
    k = x.shape[1]
    hi = x.astype(BF16)
    r1 = x - hi.astype(F32)
    mid = r1.astype(BF16)
    lo = (r1 - mid.astype(F32)).astype(BF16)
    parts = jnp.dot(tri, jnp.concatenate([hi, mid, lo], axis=1), preferred_element_type=F32)
    return (parts[:, 0:k] + parts[:, k:2 * k]) + parts[:, 2 * k:3 * k]


def _block_ref_rows(b, half):
    c, k = b.shape
    blk = 2 * half
    if blk >= 8:
        b3 = b.reshape(c // blk, blk, k)
        return jnp.broadcast_to(b3[:, half - 1:half, :], (c // blk, blk, k)).reshape(c, k)
    b3 = b.reshape(c // 8, 8, k)
    sub = lax.broadcasted_iota(jnp.int32, (c // 8, 8, k), 1)
    out = None
    for j in range(8 // blk):
        r = j * blk + half - 1
        cand = jnp.broadcast_to(b3[:, r:r + 1, :], (c // 8, 8, k))
        out = cand if out is None else jnp.where(sub >= j * blk, cand, out)
    return out.reshape(c, k)


def _gla_prompt_kernel(lb_ref, nw_ref, q_ref, f_ref, i_ref, g_ref, og_ref, st_ref, s_scr, *, layer, n_heads, kd, vd):
    c = q_ref.shape[0]
    ci = pl.program_id(1)
    n_chunks = pl.num_programs(1)

    @pl.when(ci == 0)
    def _():
        s_scr[...] = jnp.zeros_like(s_scr)

    nw = nw_ref[...]
    ti = lax.broadcasted_iota(jnp.int32, (c, c), 0)
    si = lax.broadcasted_iota(jnp.int32, (c, c), 1)
    pair_bits = jnp.where(ti > si, ti ^ si, 0)
    tri = (ti >= si).astype(BF16)
    lane = lax.broadcasted_iota(jnp.int32, (SUBLANES, c), 1)
    q_scale = kd ** -0.5
    zero_rows = {hf: jnp.zeros((hf, kd), F32) for hf in (SUBLANES << j for j in range(8)) if hf < c}

    def prep(h):
        ks = pl.ds(pl.multiple_of(h * kd, kd), kd)
        vs = pl.ds(pl.multiple_of(h * vd, vd), vd)
        lb = _lower_bound(lb_ref[:, ks], layer)
        qr = q_ref[:, ks]
        q = (qr * jax.nn.sigmoid(qr)) * q_scale
        forget = lb + (1.0 - lb) * jax.nn.sigmoid(f_ref[:, ks])
        v = i_ref[:, vs]
        b2 = _cumsum_rows(jnp.log2(forget), tri)
        return dict(h=h, vs=vs, q=q, k=1.0 - forget, forget=forget, v=v, v16=v.astype(BF16), b2=b2)

    def carry_state(t):
        q, k, b2, v16, h = t["q"], t["k"], t["b2"], t["v16"], t["h"]
        b_last = b2[c - 1:c, :]
        s_t = s_scr[h]
        t["o"] = lax.dot_general((q * jnp.exp2(b2)).astype(BF16), s_t.astype(BF16), (((1,), (1,)), ((), ())),
                                 preferred_element_type=F32)
        kdec = (k * jnp.exp2(b_last - b2)).astype(BF16)
        s_scr[h] = jnp.exp2(b_last) * s_t + lax.dot_general(v16, kdec, (((0,), (0,)), ((), ())),
                                                            preferred_element_type=F32)
        t["a"] = jnp.zeros((c, c), F32)

    def small_level(t, hf):
        q, k, b2 = t["q"], t["k"], t["b2"]
        if hf == 1:
            qe, ke = q * t["forget"], k
        else:
            e = jnp.exp2(-jnp.abs(b2 - _block_ref_rows(b2, hf)))
            qe, ke = q * e, k * e
        al = lax.dot_general(qe.astype(BF16), ke.astype(BF16), (((1,), (1,)), ((), ())),
                             preferred_element_type=F32)
        t["a"] = jnp.where(pair_bits >= hf, al, t["a"])

    def big_level(t, hf):
        q, k, b2, a_rows = t["q"], t["k"], t["b2"], t["a_rows"]
        q_parts, k_parts, uppers = [], [], []
        for blk in range(0, c, 2 * hf):
            lo, up = slice(blk, blk + hf), slice(blk + hf, blk + 2 * hf)
            r = b2[blk + hf - 1:blk + hf, :]
            q_parts.append(q[up] * jnp.exp2(b2[up] - r))
            k_parts += [k[lo] * jnp.exp2(r - b2[lo]), zero_rows[hf]]
            uppers += [(p, blk) for p in range((blk + hf) // SUBLANES, (blk + 2 * hf) // SUBLANES)]
        al = lax.dot_general(jnp.concatenate(q_parts, axis=0).astype(BF16),
                             jnp.concatenate(k_parts, axis=0).astype(BF16),
                             (((1,), (1,)), ((), ())), preferred_element_type=F32)
        for j, (p, blk) in enumerate(uppers):
            same_block = (lane >= blk) & (lane < blk + hf)
            a_rows[p] = jnp.where(same_block, al[j * SUBLANES:(j + 1) * SUBLANES], a_rows[p])

    def finish(t):
        q, k, v = t["q"], t["k"], t["v"]
        o = t["o"] + jnp.dot(jnp.concatenate(t["a_rows"], axis=0).astype(BF16), t["v16"], preferred_element_type=F32)
        o = o + jnp.sum(q * k, axis=-1, keepdims=True) * v
        o = (o * lax.rsqrt(jnp.mean(o * o, axis=-1, keepdims=True) + RMS_EPS)) * nw
        gr = g_ref[:, t["vs"]]
        og_ref[:, t["vs"]] = (o * (gr * jax.nn.sigmoid(gr))).astype(og_ref.dtype)

    group = min(GLA_HEAD_GROUP, n_heads)

    def head_group(hg, carry):
        ts = [prep(hg * group + j) for j in range(group)]
        for t in ts:
            carry_state(t)
        hf = 1
        while hf < min(SUBLANES, c):
            for t in ts:
                small_level(t, hf)
            hf *= 2
        for t in ts:
            t["a_rows"] = [t["a"][r:r + SUBLANES] for r in range(0, c, SUBLANES)]
        while hf < c:
            for t in ts:
                big_level(t, hf)
            hf *= 2
        for t in ts:
            finish(t)
        return carry

    lax.fori_loop(0, n_heads // group, head_group, 0)

    @pl.when(ci == n_chunks - 1)
    def _():
        def emit(h, carry):
            st_ref[0, h] = s_scr[h].T
            return carry

        lax.fori_loop(0, n_heads, emit, 0)


def gla_prompt(u, lower_bounds, norm_w, layer, batch, seq, n_heads, kd, vd):
    d = n_heads * kd
    c = _tile(seq, GLA_CHUNK)
    nch = seq // c
    n_layers = lower_bounds.shape[0]
    col = lambda j: pl.BlockSpec((c, d), lambda b, i: (b * nch + i, j))
    return pl.pallas_call(
        functools.partial(_gla_prompt_kernel, layer=layer, n_heads=n_heads, kd=kd, vd=vd),
        out_shape=(
            jax.ShapeDtypeStruct((batch * seq, d), BF16),
            jax.ShapeDtypeStruct((batch, n_heads, kd, vd), F32),
        ),
        grid=(batch, nch),
        in_specs=[
            pl.BlockSpec((n_layers, d), lambda b, i: (0, 0)),
            pl.BlockSpec((1, vd), lambda b, i: (0, 0)),
            col(0), col(1), col(2), col(3),
        ],
        out_specs=(
            pl.BlockSpec((c, d), lambda b, i: (b * nch + i, 0)),
            pl.BlockSpec((1, n_heads, kd, vd), lambda b, i: (b, 0, 0, 0)),
        ),
        scratch_shapes=[pltpu.VMEM((n_heads, vd, kd), F32)],
        compiler_params=_params("parallel", "arbitrary"),
        name="gla_prompt",
    )(lower_bounds, norm_w.reshape(1, vd), u, u, u, u)


def _gla_sample_kernel(lb_ref, nw_ref, x_ref, s_ref, og_ref, so_ref, *, layer, n_heads, kd):
    nh = n_heads
    x = x_ref[0]
    lb = _lower_bound(lb_ref[...], layer)[0]
    qr = x[0:nh]
    q = (qr * jax.nn.sigmoid(qr)) * (kd ** -0.5)
    forget = lb + (1.0 - lb) * jax.nn.sigmoid(x[nh:2 * nh])
    k = 1.0 - forget
    v = x[2 * nh:3 * nh]
    gr = x[3 * nh:4 * nh]
    pad = jnp.zeros((x.shape[0] - 3 * nh, kd), F32)
    cols = jnp.concatenate([q, forget, k, pad], axis=0).T
    outs = []
    for h in range(nh):
        s_new = cols[:, nh + h:nh + h + 1] * s_ref[0, h] + cols[:, 2 * nh + h:2 * nh + h + 1] * v[h:h + 1, :]
        so_ref[0, h] = s_new
        outs.append(jnp.sum(cols[:, h:h + 1] * s_new, axis=0, keepdims=True))
    o = jnp.concatenate(outs, axis=0)
    o = (o * lax.rsqrt(jnp.mean(o * o, axis=-1, keepdims=True) + RMS_EPS)) * nw_ref[...]
    og_ref[0] = o * (gr * jax.nn.sigmoid(gr))


def gla_sample(u, state, lower_bounds, norm_w, layer, n_heads, kd, vd):
    nb = u.shape[0]
    n_layers = lower_bounds.shape[0]
    rows = 4 * n_heads
    return pl.pallas_call(
        functools.partial(_gla_sample_kernel, layer=layer, n_heads=n_heads, kd=kd),
        out_shape=(
            jax.ShapeDtypeStruct((nb, n_heads, vd), F32),
            jax.ShapeDtypeStruct((nb, n_heads, kd, vd), F32),
        ),
        grid=(nb,),
        in_specs=[
            pl.BlockSpec((n_layers, n_heads, kd), lambda i: (0, 0, 0)),
            pl.BlockSpec((1, vd), lambda i: (0, 0)),
            pl.BlockSpec((1, rows, kd), lambda i: (i, 0, 0)),
            pl.BlockSpec((1, n_heads, kd, vd), lambda i: (i, 0, 0, 0)),
        ],
        out_specs=(
            pl.BlockSpec((1, n_heads, vd), lambda i: (i, 0, 0)),
            pl.BlockSpec((1, n_heads, kd, vd), lambda i: (i, 0, 0, 0)),
        ),
        compiler_params=_params("parallel"),
        name="gla_sample",
    )(lower_bounds.reshape(n_layers, n_heads, kd), norm_w.reshape(1, vd), u.reshape(nb, rows, kd), state)


def kernel(x_prompt, x_sample, cache_k, cache_v, state_hgrn, norm_mix, norm_ffn, norm_final, attn_w_qkv, attn_b_qkv,
           attn_sinks, attn_w_o, attn_b_o, hgrn_w_in, hgrn_lower_bounds, hgrn_norm, hgrn_w_o, ffn_w_in, ffn_w_out):
    batch, seq, d = x_prompt.shape
    nb = x_sample.shape[0]
    assert x_sample.shape[1] == 1
    depth = norm_mix.shape[0]
    window, n_kv, hd = cache_k.shape[2:]
    n_q = attn_sinks.shape[1]
    q_per_kv = n_q // n_kv
    q_dim, kv_dim = n_q * hd, n_kv * hd
    n_heads, kd, vd = state_hgrn.shape[2:]

    hp = x_prompt.reshape(batch * seq, d)
    hs = x_sample.reshape(nb, d)
    wk_p, wv_p, st_p, wk_s, wv_s, st_s = [], [], [], [], [], []
    for i in range(depth):
        if i % 2 == 0:
            a = i // 2
            qkv_p, qkv_s = matmul(hp, hs, attn_w_qkv, a, norm_w=norm_mix[i], bias=attn_b_qkv[a])
            op = attn_prompt(qkv_p, attn_sinks[a], batch, seq, n_kv, q_per_kv, hd, window)
            kv_p = qkv_p.reshape(batch, seq, q_dim + 2 * kv_dim)[:, seq - window:, q_dim:]
            wk_p.append(kv_p[..., :kv_dim].reshape(batch, window, n_kv, hd))
            wv_p.append(kv_p[..., kv_dim:].reshape(batch, window, n_kv, hd))
            os_, wk, wv = attn_sample(
                qkv_s[:, :q_dim].reshape(nb, n_q, hd),
                qkv_s[:, q_dim:q_dim + kv_dim].reshape(nb, 1, kv_dim),
                qkv_s[:, q_dim + kv_dim:].reshape(nb, 1, kv_dim),
                cache_k[a].reshape(nb, window, kv_dim), cache_v[a].reshape(nb, window, kv_dim),
                attn_sinks[a], n_kv, q_per_kv, hd, window)
            wk_s.append(wk.reshape(nb, window, n_kv, hd))
            wv_s.append(wv.reshape(nb, window, n_kv, hd))
            hp, hs = matmul(op, os_.reshape(nb, q_dim).astype(BF16), attn_w_o, a, bias=attn_b_o[a], res=hp, res_s=hs)
        else:
            r = i // 2
            up, us = matmul(hp, hs, hgrn_w_in, r, norm_w=norm_mix[i])
            ogp, sp = gla_prompt(up, hgrn_lower_bounds, hgrn_norm[r], i, batch, seq, n_heads, kd, vd)
            ogs, ss = gla_sample(us, state_hgrn[r], hgrn_lower_bounds, hgrn_norm[r], i, n_heads, kd, vd)
            st_p.append(sp)
            st_s.append(ss)
            hp, hs = matmul(ogp, ogs.reshape(nb, d).astype(BF16), hgrn_w_o, r, res=hp, res_s=hs)
        final_w = norm_final if i == depth - 1 else None
        hs, wg, wu, wo = ffn_sample(hs, norm_ffn[i], ffn_w_in, ffn_w_out, i, final_w=final_w)
        hp = ffn(hp, norm_ffn[i], wg, wu, wo, final_w=final_w)

    y_prompt = hp.reshape(batch, seq, d)
    y_sample = hs.reshape(nb, 1, d)
    return (y_prompt, y_sample, jnp.stack(wk_p), jnp.stack(wv_p), jnp.stack(st_p), jnp.stack(wk_s), jnp.stack(wv_s),
            jnp.stack(st_s))
```

```python
import functools

import jax
import jax.numpy as jnp
from jax import lax
from jax.experimental import pallas as pl
from jax.experimental.pallas import tpu as pltpu

RMS_EPS = 1e-5
BF16 = jnp.bfloat16
F32 = jnp.float32

V7X_VMEM_LIMIT_BYTES = 56 * 1024 * 1024
GLA_CHUNK = 128
GLA_HEAD_GROUP = 8
SUBLANES = 8
LANES = 128
LOG2E = 1.4426950408889634


def _params(*sem):
    return pltpu.CompilerParams(dimension_semantics=sem, vmem_limit_bytes=V7X_VMEM_LIMIT_BYTES)


def _tile(n, pref):
    return pref if n % pref == 0 else n


def _rmsnorm_rows(x, w):
    ms = jnp.mean(x * x, axis=-1, keepdims=True)
    return (x * lax.rsqrt(ms + RMS_EPS)) * w


NORM_ROWS = 16
NORM_UNROLL = 4


def _norm_rows_into(src_ref, nw_ref, dst_ref, copy_ref=None):
    total = src_ref.shape[0]
    rows = min(NORM_ROWS, total)

    def body(r, carry):
        rs = pl.ds(pl.multiple_of(r * rows, rows), rows)
        h = src_ref[rs, :]
        dst_ref[rs, :] = _rmsnorm_rows(h, nw_ref[...]).astype(BF16)
        if copy_ref is not None:
            copy_ref[rs, :] = h
        return carry

    lax.fori_loop(0, total // rows, body, 0, unroll=min(NORM_UNROLL, total // rows))


def _matmul_kernel(*refs, has_norm, has_bias, has_res):
    x_ref, xs_ref, w_ref = refs[0], refs[1], refs[2]
    pos = 3
    nw_ref = b_ref = r_ref = rs_ref = None
    if has_norm:
        nw_ref = refs[pos]
        pos += 1
    if has_bias:
        b_ref = refs[pos]
        pos += 1
    if has_res:
        r_ref, rs_ref = refs[pos], refs[pos + 1]
        pos += 2
    o_ref, os_ref = refs[pos], refs[pos + 1]
    i, j = pl.program_id(0), pl.program_id(1)
    if has_norm:
        xn_ref, xsn_ref = refs[pos + 2], refs[pos + 3]

        @pl.when(j == 0)
        def _():
            _norm_rows_into(x_ref, nw_ref, xn_ref)

        @pl.when((j == 0) & (i == 0))
        def _():
            _norm_rows_into(xs_ref, nw_ref, xsn_ref)
    else:
        xn_ref, xsn_ref = x_ref, xs_ref
    w = w_ref[...].astype(BF16)

    def contract(lhs_ref):
        return jnp.dot(lhs_ref[...], w, preferred_element_type=F32)

    def finish(acc, res_ref, out_ref):
        if has_bias:
            acc = acc + b_ref[...]
        if has_res:
            acc = acc + res_ref[...]
        out_ref[...] = acc.astype(out_ref.dtype)

    finish(contract(xn_ref), r_ref, o_ref)

    @pl.when(i == 0)
    def _():
        finish(contract(xsn_ref), rs_ref, os_ref)


def matmul(x, xs, w, layer, norm_w=None, bias=None, res=None, res_s=None, out_dtype=F32):
    m, k = x.shape
    ms = xs.shape[0]
    n = w.shape[2]
    tm = _tile(m, 1024)
    tn = _tile(n, 512)
    nj = n // tn
    js = lambda i, j: jnp.where(i == 0, j, nj - 1)
    has_norm = norm_w is not None
    once = pl.Buffered(1)
    in_specs = [
        pl.BlockSpec((tm, k), lambda i, j: (i, 0), **({"pipeline_mode": once} if has_norm else {})),
        pl.BlockSpec((ms, k), lambda i, j: (0, 0)),
        pl.BlockSpec((None, k, tn), lambda i, j: (layer, 0, j)),
    ]
    args = [x, xs, w]
    scratch = []
    if has_norm:
        in_specs.append(pl.BlockSpec((1, k), lambda i, j: (0, 0)))
        args.append(norm_w.reshape(1, k))
        scratch = [pltpu.VMEM((tm, k), BF16), pltpu.VMEM((ms, k), BF16)]
    if bias is not None:
        in_specs.append(pl.BlockSpec((1, tn), lambda i, j: (0, j)))
        args.append(bias.reshape(1, n))
    if res is not None:
        in_specs.append(pl.BlockSpec((tm, tn), lambda i, j: (i, j)))
        in_specs.append(pl.BlockSpec((ms, tn), lambda i, j: (0, js(i, j))))
        args += [res, res_s]
    return pl.pallas_call(
        functools.partial(_matmul_kernel, has_norm=has_norm, has_bias=bias is not None, has_res=res is not None),
        out_shape=(jax.ShapeDtypeStruct((m, n), out_dtype), jax.ShapeDtypeStruct((ms, n), out_dtype)),
        grid=(m // tm, nj),
        in_specs=in_specs,
        out_specs=(pl.BlockSpec((tm, tn), lambda i, j: (i, j)), pl.BlockSpec((ms, tn), lambda i, j: (0, js(i, j)))),
        scratch_shapes=scratch,
        compiler_params=_params("arbitrary", "arbitrary"),
        name="matmul",
    )(*args)


FFN_ROWS = 1024
FFN_OUT_COLS = 512
V7X_FFN_VMEM_LIMIT_BYTES = 60 * 1024 * 1024


def _final_norm_in_place(o_ref, fw_ref):
    total = o_ref.shape[0]
    rows = min(NORM_ROWS, total)

    def body(r, carry):
        rs = pl.ds(pl.multiple_of(r * rows, rows), rows)
        o_ref[rs, :] = _rmsnorm_rows(o_ref[rs, :], fw_ref[...])
        return carry

    lax.fori_loop(0, total // rows, body, 0, unroll=min(NORM_UNROLL, total // rows))


def _ffn_kernel(*refs, final_norm):
    h_hbm, nw_ref = refs[0], refs[1]
    fw_ref = refs[2] if final_norm else None
    wg_ref, wu_ref, wo_ref, o_ref, xn_ref, h_sem = refs[-6:]
    i, f = pl.program_id(0), pl.program_id(1)
    tm, d = o_ref.shape

    @pl.when(f == 0)
    def _():
        h_copy = pltpu.make_async_copy(h_hbm.at[pl.ds(pl.multiple_of(i * tm, tm), tm), :], o_ref, h_sem)
        h_copy.start()
        h_copy.wait()
        _norm_rows_into(o_ref, nw_ref, xn_ref)

    xn = xn_ref[...]
    g = jnp.dot(xn, wg_ref[...].astype(BF16), preferred_element_type=F32)
    u = jnp.dot(xn, wu_ref[...].astype(BF16), preferred_element_type=F32)
    act = ((g * jax.nn.sigmoid(g)) * u).astype(BF16)
    for c in range(0, d, FFN_OUT_COLS):
        cs = slice(c, min(c + FFN_OUT_COLS, d))
        o_ref[:, cs] += jnp.dot(act, wo_ref[:, cs].astype(BF16), preferred_element_type=F32)
    if final_norm:
        @pl.when(f == pl.num_programs(1) - 1)
        def _():
            _final_norm_in_place(o_ref, fw_ref)


def ffn(h, norm_w, w_in, w_out, layer, final_w=None):
    m, d = h.shape
    d_ff = w_out.shape[1]
    tm = _tile(m, FFN_ROWS)
    tf = _tile(d_ff, 256)
    nf = d_ff // tf
    once = pl.Buffered(1)
    vec = pl.BlockSpec((1, d), lambda i, f: (0, 0), pipeline_mode=once)
    final_norm = final_w is not None
    return pl.pallas_call(
        functools.partial(_ffn_kernel, final_norm=final_norm),
        out_shape=jax.ShapeDtypeStruct((m, d), F32),
        grid=(m // tm, nf),
        in_specs=[pl.BlockSpec(memory_space=pl.ANY), vec] + [vec] * final_norm + [
            pl.BlockSpec((None, d, tf), lambda i, f: (layer, 0, f)),
            pl.BlockSpec((None, d, tf), lambda i, f: (layer, 0, f + nf)),
            pl.BlockSpec((None, tf, d), lambda i, f: (layer, f, 0)),
        ],
        out_specs=pl.BlockSpec((tm, d), lambda i, f: (i, 0), pipeline_mode=once),
        scratch_shapes=[pltpu.VMEM((tm, d), BF16), pltpu.SemaphoreType.DMA(())],
        compiler_params=pltpu.CompilerParams(
            dimension_semantics=("arbitrary", "arbitrary"), vmem_limit_bytes=V7X_FFN_VMEM_LIMIT_BYTES),
        name="ffn",
    )(h, norm_w.reshape(1, d), *([final_w.reshape(1, d)] if final_norm else []), w_in, w_in, w_out)


def _attn_prompt_kernel(sink_ref, q_ref, kc_ref, kp_ref, vc_ref, vp_ref, o_ref, *, n_kv, q_per_kv, hd, window):
    nb = pl.program_id(1)
    w = window
    assert 2 * hd == LANES and q_per_kv % 2 == 0 and n_kv % 2 == 0
    npair = q_per_kv // 2
    cols = npair * w
    q_low = lax.broadcasted_iota(jnp.int32, (w, LANES), 1) < hd
    kv_low = lax.broadcasted_iota(jnp.int32, (2 * w, LANES), 1) < hd
    kv_row0 = lax.broadcasted_iota(jnp.int32, (2 * w, LANES), 0) == 0
    kj = lax.broadcasted_iota(jnp.int32, (w, cols), 0)
    qi = lax.broadcasted_iota(jnp.int32, (w, cols), 1) & (w - 1)
    from_prev = kj > qi
    top_row0 = lax.broadcasted_iota(jnp.int32, (SUBLANES, cols), 0) == 0
    no_prev = jnp.where(nb > 0, 0.0, -jnp.inf)
    q_scale = hd ** -0.5 * LOG2E
    for h in range(n_kv):
        slab, half = divmod(h, 2)
        ksl = slice(slab * LANES, (slab + 1) * LANES)
        kk = [jnp.concatenate([kp_ref[:, ksl], kc_ref[:, ksl]], axis=0)]
        vv = [jnp.concatenate([vp_ref[:, ksl], vc_ref[:, ksl]], axis=0)]
        kk.append(pltpu.roll(kk[0], hd, axis=1))
        vv.append(pltpu.roll(vv[0], hd, axis=1))
        res = []
        for par in range(2):
            keep = q_low if par == 0 else ~q_low
            q = jnp.concatenate(
                [jnp.where(keep, q_ref[:, (h * npair + p) * LANES:(h * npair + p + 1) * LANES] * q_scale, 0.0)
                 for p in range(npair)], axis=0).astype(BF16)
            swap = int(par != half)
            s = lax.dot_general(kk[swap].astype(BF16), q, (((1,), (1,)), ((), ())), preferred_element_type=F32)
            s = jnp.where(from_prev, s[:w] + no_prev, s[w:])
            sink = jnp.concatenate(
                [jnp.full((1, w), sink_ref[h * q_per_kv + 2 * p + par] * LOG2E, F32) for p in range(npair)], axis=1)
            m = jnp.maximum(jnp.max(s, axis=0, keepdims=True), sink)
            e = jnp.exp2(s - m)
            e_prev = jnp.where(from_prev, e, 0.0)
            e_top = jnp.where(top_row0, jnp.exp2(sink - m), e_prev[:SUBLANES])
            e2 = jnp.concatenate([e_top, e_prev[SUBLANES:], jnp.where(from_prev, 0.0, e)], axis=0).astype(BF16)
            own = kv_low if par == 0 else ~kv_low
            v_aug = jnp.where(own, jnp.where(kv_row0, 0.0, vv[swap]), 1.0).astype(BF16)
            res.append(lax.dot_general(e2, v_aug, (((0,), (0,)), ((), ())), preferred_element_type=F32))
        for p in range(npair):
            r0, r1 = res[0][p * w:(p + 1) * w], res[1][p * w:(p + 1) * w]
            num = jnp.where(q_low, r0, r1)
            den = pltpu.roll(jnp.where(q_low, r1, r0), hd, axis=1)
            c = (h * npair + p) * LANES
            o_ref[:, c:c + LANES] = (num * (1.0 / den)).astype(o_ref.dtype)


def attn_prompt(qkv, sinks, batch, seq, n_kv, q_per_kv, hd, window):
    q_dim = n_kv * q_per_kv * hd
    kv_dim = n_kv * hd
    nblk = seq // window
    kcol = q_dim // kv_dim
    cur = lambda b, n: b * nblk + n
    prev = lambda b, n: b * nblk + jnp.maximum(n - 1, 0)
    return pl.pallas_call(
        functools.partial(_attn_prompt_kernel, n_kv=n_kv, q_per_kv=q_per_kv, hd=hd, window=window),
        out_shape=jax.ShapeDtypeStruct((batch * seq, q_dim), BF16),
        grid=(batch, nblk),
        in_specs=[
            pl.BlockSpec(memory_space=pltpu.SMEM),
            pl.BlockSpec((window, q_dim), lambda b, n: (cur(b, n), 0)),
            pl.BlockSpec((window, kv_dim), lambda b, n: (cur(b, n), kcol)),
            pl.BlockSpec((window, kv_dim), lambda b, n: (prev(b, n), kcol)),
            pl.BlockSpec((window, kv_dim), lambda b, n: (cur(b, n), kcol + 1)),
            pl.BlockSpec((window, kv_dim), lambda b, n: (prev(b, n), kcol + 1)),
        ],
        out_specs=pl.BlockSpec((window, q_dim), lambda b, n: (cur(b, n), 0)),
        compiler_params=_params("parallel", "arbitrary"),
        name="attn_prompt",
    )(sinks, qkv, qkv, qkv, qkv, qkv)


def _attn_sample_kernel(sink_ref, q_ref, kn_ref, vn_ref, ck_ref, cv_ref, o_ref, wk_ref, wv_ref, *, n_kv, q_per_kv, hd,
                        window):
    w = window
    wk_ref[0, 0:w - 1, :] = ck_ref[0, 1:w, :]
    wk_ref[0, w - 1:w, :] = kn_ref[0]
    wv_ref[0, 0:w - 1, :] = cv_ref[0, 1:w, :]
    wv_ref[0, w - 1:w, :] = vn_ref[0]
    scale = hd ** -0.5
    for h in range(n_kv):
        rs = slice(h * q_per_kv, (h + 1) * q_per_kv)
        ks = slice(h * hd, (h + 1) * hd)
        q = (q_ref[0, rs, :] * scale).astype(BF16)
        kk = wk_ref[0, :, ks].astype(BF16)
        vv = wv_ref[0, :, ks].astype(BF16)
        s = lax.dot_general(q, kk, (((1,), (1,)), ((), ())), preferred_element_type=F32)
        sink = sink_ref[rs, :]
        m = jnp.maximum(jnp.max(s, axis=-1, keepdims=True), sink)
        e = jnp.exp(s - m)
        den = jnp.sum(e, axis=-1, keepdims=True) + jnp.exp(sink - m)
        o_ref[0, rs, :] = jnp.dot(e.astype(BF16), vv, preferred_element_type=F32) * (1.0 / den)


def attn_sample(q, k_new, v_new, cache_k, cache_v, sinks, n_kv, q_per_kv, hd, window):
    nb = q.shape[0]
    nq = n_kv * q_per_kv
    kv_dim = n_kv * hd
    win = pl.BlockSpec((1, window, kv_dim), lambda i: (i, 0, 0))
    row = pl.BlockSpec((1, 1, kv_dim), lambda i: (i, 0, 0))
    qs = pl.BlockSpec((1, nq, hd), lambda i: (i, 0, 0))
    return pl.pallas_call(
        functools.partial(_attn_sample_kernel, n_kv=n_kv, q_per_kv=q_per_kv, hd=hd, window=window),
        out_shape=(
            jax.ShapeDtypeStruct((nb, nq, hd), F32),
            jax.ShapeDtypeStruct((nb, window, kv_dim), F32),
            jax.ShapeDtypeStruct((nb, window, kv_dim), F32),
        ),
        grid=(nb,),
        in_specs=[pl.BlockSpec((nq, 1), lambda i: (0, 0)), qs, row, row, win, win],
        out_specs=(qs, win, win),
        compiler_params=_params("parallel"),
        name="attn_sample",
    )(sinks.reshape(nq, 1), q, k_new, v_new, cache_k, cache_v)


def _lower_bound(l, layer):
    e = jnp.exp(l - jnp.max(l, axis=0, keepdims=True))
    p = e / jnp.sum(e, axis=0, keepdims=True)
    return jnp.sum(p[0:layer + 1], axis=0, keepdims=True) - p[0:1]


def _cumsum_rows(x, tri):
    k = x.shape[1]
    hi = x.astype(BF16)
    r1 = x - hi.astype(F32)
    mid = r1.astype(BF16)
    lo = (r1 - mid.astype(F32)).astype(BF16)
    parts = jnp.dot(tri, jnp.concatenate([hi, mid, lo], axis=1), preferred_element_type=F32)
    return (parts[:, 0:k] + parts[:, k:2 * k]) + parts[:, 2 * k:3 * k]


def _block_ref_rows(b, half):
    c, k = b.shape
    blk = 2 * half
    if blk >= 8:
        b3 = b.reshape(c // blk, blk, k)
        return jnp.broadcast_to(b3[:, half - 1:half, :], (c // blk, blk, k)).reshape(c, k)
    b3 = b.reshape(c // 8, 8, k)
    sub = lax.broadcasted_iota(jnp.int32, (c // 8, 8, k), 1)
    out = None
    for j in range(8 // blk):
        r = j * blk + half - 1
        cand = jnp.broadcast_to(b3[:, r:r + 1, :], (c // 8, 8, k))
        out = cand if out is None else jnp.where(sub >= j * blk, cand, out)
    return out.reshape(c, k)


def _gla_prompt_kernel(lb_ref, nw_ref, q_ref, f_ref, i_ref, g_ref, og_ref, st_ref, s_scr, *, layer, n_heads, kd, vd):
    c = q_ref.shape[0]
    ci = pl.program_id(1)
    n_chunks = pl.num_programs(1)

    @pl.when(ci == 0)
    def _():
        s_scr[...] = jnp.zeros_like(s_scr)

    nw = nw_ref[...]
    ti = lax.broadcasted_iota(jnp.int32, (c, c), 0)
    si = lax.broadcasted_iota(jnp.int32, (c, c), 1)
    pair_bits = jnp.where(ti > si, ti ^ si, 0)
    tri = (ti >= si).astype(BF16)
    lane = lax.broadcasted_iota(jnp.int32, (SUBLANES, c), 1)
    q_scale = kd ** -0.5
    zero_rows = {hf: jnp.zeros((hf, kd), F32) for hf in (SUBLANES << j for j in range(8)) if hf < c}

    def prep(h):
        ks = pl.ds(pl.multiple_of(h * kd, kd), kd)
        vs = pl.ds(pl.multiple_of(h * vd, vd), vd)
        lb = _lower_bound(lb_ref[:, ks], layer)
        qr = q_ref[:, ks]
        q = (qr * jax.nn.sigmoid(qr)) * q_scale
        forget = lb + (1.0 - lb) * jax.nn.sigmoid(f_ref[:, ks])
        v = i_ref[:, vs]
        b2 = _cumsum_rows(jnp.log2(forget), tri)
        return dict(h=h, vs=vs, q=q, k=1.0 - forget, forget=forget, v=v, v16=v.astype(BF16), b2=b2)

    def carry_state(t):
        q, k, b2, v16, h = t["q"], t["k"], t["b2"], t["v16"], t["h"]
        b_last = b2[c - 1:c, :]
        s_t = s_scr[h]
        t["o"] = lax.dot_general((q * jnp.exp2(b2)).astype(BF16), s_t.astype(BF16), (((1,), (1,)), ((), ())),
                                 preferred_element_type=F32)
        kdec = (k * jnp.exp2(b_last - b2)).astype(BF16)
        s_scr[h] = jnp.exp2(b_last) * s_t + lax.dot_general(v16, kdec, (((0,), (0,)), ((), ())),
                                                            preferred_element_type=F32)
        t["a"] = jnp.zeros((c, c), F32)

    def small_level(t, hf):
        q, k, b2 = t["q"], t["k"], t["b2"]
        if hf == 1:
            qe, ke = q * t["forget"], k
        else:
            e = jnp.exp2(-jnp.abs(b2 - _block_ref_rows(b2, hf)))
            qe, ke = q * e, k * e
        al = lax.dot_general(qe.astype(BF16), ke.astype(BF16), (((1,), (1,)), ((), ())),
                             preferred_element_type=F32)
        t["a"] = jnp.where(pair_bits >= hf, al, t["a"])

    def big_level(t, hf):
        q, k, b2, a_rows = t["q"], t["k"], t["b2"], t["a_rows"]
        q_parts, k_parts, uppers = [], [], []
        for blk in range(0, c, 2 * hf):
            lo, up = slice(blk, blk + hf), slice(blk + hf, blk + 2 * hf)
            r = b2[blk + hf - 1:blk + hf, :]
            q_parts.append(q[up] * jnp.exp2(b2[up] - r))
            k_parts += [k[lo] * jnp.exp2(r - b2[lo]), zero_rows[hf]]
            uppers += [(p, blk) for p in range((blk + hf) // SUBLANES, (blk + 2 * hf) // SUBLANES)]
        al = lax.dot_general(jnp.concatenate(q_parts, axis=0).astype(BF16),
                             jnp.concatenate(k_parts, axis=0).astype(BF16),
                             (((1,), (1,)), ((), ())), preferred_element_type=F32)
        for j, (p, blk) in enumerate(uppers):
            same_block = (lane >= blk) & (lane < blk + hf)
            a_rows[p] = jnp.where(same_block, al[j * SUBLANES:(j + 1) * SUBLANES], a_rows[p])

    def finish(t):
        q, k, v = t["q"], t["k"], t["v"]
        o = t["o"] + jnp.dot(jnp.concatenate(t["a_rows"], axis=0).astype(BF16), t["v16"], preferred_element_type=F32)
        o = o + jnp.sum(q * k, axis=-1, keepdims=True) * v
        o = (o * lax.rsqrt(jnp.mean(o * o, axis=-1, keepdims=True) + RMS_EPS)) * nw
        gr = g_ref[:, t["vs"]]
        og_ref[:, t["vs"]] = (o * (gr * jax.nn.sigmoid(gr))).astype(og_ref.dtype)

    group = min(GLA_HEAD_GROUP, n_heads)

    def head_group(hg, carry):
        ts = [prep(hg * group + j) for j in range(group)]
        for t in ts:
            carry_state(t)
        hf = 1
        while hf < min(SUBLANES, c):
            for t in ts:
                small_level(t, hf)
            hf *= 2
        for t in ts:
            t["a_rows"] = [t["a"][r:r + SUBLANES] for r in range(0, c, SUBLANES)]
        while hf < c:
            for t in ts:
                big_level(t, hf)
            hf *= 2
        for t in ts:
            finish(t)
        return carry

    lax.fori_loop(0, n_heads // group, head_group, 0)

    @pl.when(ci == n_chunks - 1)
    def _():
        def emit(h, carry):
            st_ref[0, h] = s_scr[h].T
            return carry

        lax.fori_loop(0, n_heads, emit, 0)


def gla_prompt(u, lower_bounds, norm_w, layer, batch, seq, n_heads, kd, vd):
    d = n_heads * kd
    c = _tile(seq, GLA_CHUNK)
    nch = seq // c
    n_layers = lower_bounds.shape[0]
    col = lambda j: pl.BlockSpec((c, d), lambda b, i: (b * nch + i, j))
    return pl.pallas_call(
        functools.partial(_gla_prompt_kernel, layer=layer, n_heads=n_heads, kd=kd, vd=vd),
        out_shape=(
            jax.ShapeDtypeStruct((batch * seq, d), BF16),
            jax.ShapeDtypeStruct((batch, n_heads, kd, vd), F32),
        ),
        grid=(batch, nch),
        in_specs=[
            pl.BlockSpec((n_layers, d), lambda b, i: (0, 0)),
            pl.BlockSpec((1, vd), lambda b, i: (0, 0)),
            col(0), col(1), col(2), col(3),
        ],
        out_specs=(
            pl.BlockSpec((c, d), lambda b, i: (b * nch + i, 0)),
            pl.BlockSpec((1, n_heads, kd, vd), lambda b, i: (b, 0, 0, 0)),
        ),
        scratch_shapes=[pltpu.VMEM((n_heads, vd, kd), F32)],
        compiler_params=_params("parallel", "arbitrary"),
        name="gla_prompt",
    )(lower_bounds, norm_w.reshape(1, vd), u, u, u, u)


def _gla_sample_kernel(lb_ref, nw_ref, x_ref, s_ref, og_ref, so_ref, *, layer, n_heads, kd):
    nh = n_heads
    x = x_ref[0]
    lb = _lower_bound(lb_ref[...], layer)[0]
    qr = x[0:nh]
    q = (qr * jax.nn.sigmoid(qr)) * (kd ** -0.5)
    forget = lb + (1.0 - lb) * jax.nn.sigmoid(x[nh:2 * nh])
    k = 1.0 - forget
    v = x[2 * nh:3 * nh]
    gr = x[3 * nh:4 * nh]
    pad = jnp.zeros((x.shape[0] - 3 * nh, kd), F32)
    cols = jnp.concatenate([q, forget, k, pad], axis=0).T
    outs = []
    for h in range(nh):
        s_new = cols[:, nh + h:nh + h + 1] * s_ref[0, h] + cols[:, 2 * nh + h:2 * nh + h + 1] * v[h:h + 1, :]
        so_ref[0, h] = s_new
        outs.append(jnp.sum(cols[:, h:h + 1] * s_new, axis=0, keepdims=True))
    o = jnp.concatenate(outs, axis=0)
    o = (o * lax.rsqrt(jnp.mean(o * o, axis=-1, keepdims=True) + RMS_EPS)) * nw_ref[...]
    og_ref[0] = o * (gr * jax.nn.sigmoid(gr))


def gla_sample(u, state, lower_bounds, norm_w, layer, n_heads, kd, vd):
    nb = u.shape[0]
    n_layers = lower_bounds.shape[0]
    rows = 4 * n_heads
    return pl.pallas_call(
        functools.partial(_gla_sample_kernel, layer=layer, n_heads=n_heads, kd=kd),
        out_shape=(
            jax.ShapeDtypeStruct((nb, n_heads, vd), F32),
            jax.ShapeDtypeStruct((nb, n_heads, kd, vd), F32),
        ),
        grid=(nb,),
        in_specs=[
            pl.BlockSpec((n_layers, n_heads, kd), lambda i: (0, 0, 0)),
            pl.BlockSpec((1, vd), lambda i: (0, 0)),
            pl.BlockSpec((1, rows, kd), lambda i: (i, 0, 0)),
            pl.BlockSpec((1, n_heads, kd, vd), lambda i: (i, 0, 0, 0)),
        ],
        out_specs=(
            pl.BlockSpec((1, n_heads, vd), lambda i: (i, 0, 0)),
            pl.BlockSpec((1, n_heads, kd, vd), lambda i: (i, 0, 0, 0)),
        ),
        compiler_params=_params("parallel"),
        name="gla_sample",
    )(lower_bounds.reshape(n_layers, n_heads, kd), norm_w.reshape(1, vd), u.reshape(nb, rows, kd), state)


def kernel(x_prompt, x_sample, cache_k, cache_v, state_hgrn, norm_mix, norm_ffn, norm_final, attn_w_qkv, attn_b_qkv,
           attn_sinks, attn_w_o, attn_b_o, hgrn_w_in, hgrn_lower_bounds, hgrn_norm, hgrn_w_o, ffn_w_in, ffn_w_out):
    batch, seq, d = x_prompt.shape
    nb = x_sample.shape[0]
    assert x_sample.shape[1] == 1
    depth = norm_mix.shape[0]
    window, n_kv, hd = cache_k.shape[2:]
    n_q = attn_sinks.shape[1]
    q_per_kv = n_q // n_kv
    q_dim, kv_dim = n_q * hd, n_kv * hd
    n_heads, kd, vd = state_hgrn.shape[2:]

    hp = x_prompt.reshape(batch * seq, d)
    hs = x_sample.reshape(nb, d)
    wk_p, wv_p, st_p, wk_s, wv_s, st_s = [], [], [], [], [], []
    for i in range(depth):
        if i % 2 == 0:
            a = i // 2
            qkv_p, qkv_s = matmul(hp, hs, attn_w_qkv, a, norm_w=norm_mix[i], bias=attn_b_qkv[a])
            op = attn_prompt(qkv_p, attn_sinks[a], batch, seq, n_kv, q_per_kv, hd, window)
            kv_p = qkv_p.reshape(batch, seq, q_dim + 2 * kv_dim)[:, seq - window:, q_dim:]
            wk_p.append(kv_p[..., :kv_dim].reshape(batch, window, n_kv, hd))
            wv_p.append(kv_p[..., kv_dim:].reshape(batch, window, n_kv, hd))
            os_, wk, wv = attn_sample(
                qkv_s[:, :q_dim].reshape(nb, n_q, hd),
                qkv_s[:, q_dim:q_dim + kv_dim].reshape(nb, 1, kv_dim),
                qkv_s[:, q_dim + kv_dim:].reshape(nb, 1, kv_dim),
                cache_k[a].reshape(nb, window, kv_dim), cache_v[a].reshape(nb, window, kv_dim),
                attn_sinks[a], n_kv, q_per_kv, hd, window)
            wk_s.append(wk.reshape(nb, window, n_kv, hd))
            wv_s.append(wv.reshape(nb, window, n_kv, hd))
            hp, hs = matmul(op, os_.reshape(nb, q_dim).astype(BF16), attn_w_o, a, bias=attn_b_o[a], res=hp, res_s=hs)
        else:
            r = i // 2
            up, us = matmul(hp, hs, hgrn_w_in, r, norm_w=norm_mix[i])
            ogp, sp = gla_prompt(up, hgrn_lower_bounds, hgrn_norm[r], i, batch, seq, n_heads, kd, vd)
            ogs, ss = gla_sample(us, state_hgrn[r], hgrn_lower_bounds, hgrn_norm[r], i, n_heads, kd, vd)
            st_p.append(sp)
            st_s.append(ss)
            hp, hs = matmul(ogp, ogs.reshape(nb, d).astype(BF16), hgrn_w_o, r, res=hp, res_s=hs)
        final_w = norm_final if i == depth - 1 else None
        hs = ffn(hs, norm_ffn[i], ffn_w_in, ffn_w_out, i, final_w=final_w)
        hp = ffn(hp, norm_ffn[i], ffn_w_in, ffn_w_out, i, final_w=final_w)

    y_prompt = hp.reshape(batch, seq, d)
    y_sample = hs.reshape(nb, 1, d)
    return (y_prompt, y_sample, jnp.stack(wk_p), jnp.stack(wv_p), jnp.stack(st_p), jnp.stack(wk_s), jnp.stack(wv_s),
            jnp.stack(st_s))
```

```python
import functools

import jax
import jax.numpy as jnp
from jax import lax
from jax.experimental import pallas as pl
from jax.experimental.pallas import tpu as pltpu

RMS_EPS = 1e-5
BF16 = jnp.bfloat16
F32 = jnp.float32

V7X_VMEM_LIMIT_BYTES = 56 * 1024 * 1024
GLA_CHUNK = 128
GLA_HEAD_GROUP = 8
SUBLANES = 8
BF16_SUBLANES = 16
LANES = 128
LOG2E = 1.4426950408889634


def _params(*sem):
    return pltpu.CompilerParams(dimension_semantics=sem, vmem_limit_bytes=V7X_VMEM_LIMIT_BYTES)


def _tile(n, pref):
    return pref if n % pref == 0 else n


def _rmsnorm_rows(x, w):
    ms = jnp.mean(x * x, axis=-1, keepdims=True)
    return (x * lax.rsqrt(ms + RMS_EPS)) * w


NORM_ROWS = 16
NORM_UNROLL = 4


def _norm_rows_into(src_ref, nw_ref, dst_ref, copy_ref=None):
    total = src_ref.shape[0]
    rows = min(NORM_ROWS, total)

    def body(r, carry):
        rs = pl.ds(pl.multiple_of(r * rows, rows), rows)
        h = src_ref[rs, :]
        dst_ref[rs, :] = _rmsnorm_rows(h, nw_ref[...]).astype(BF16)
        if copy_ref is not None:
            copy_ref[rs, :] = h
        return carry

    lax.fori_loop(0, total // rows, body, 0, unroll=min(NORM_UNROLL, total // rows))


def _matmul_kernel(*refs, has_norm, has_bias, has_res):
    x_ref, xs_ref, w_ref = refs[0], refs[1], refs[2]
    pos = 3
    nw_ref = b_ref = r_ref = rs_ref = None
    if has_norm:
        nw_ref = refs[pos]
        pos += 1
    if has_bias:
        b_ref = refs[pos]
        pos += 1
    if has_res:
        r_ref, rs_ref = refs[pos], refs[pos + 1]
        pos += 2
    o_ref, os_ref = refs[pos], refs[pos + 1]
    i, j = pl.program_id(0), pl.program_id(1)
    if has_norm:
        xn_ref, xsn_ref = refs[pos + 2], refs[pos + 3]

        @pl.when(j == 0)
        def _():
            _norm_rows_into(x_ref, nw_ref, xn_ref)

        @pl.when((j == 0) & (i == 0))
        def _():
            _norm_rows_into(xs_ref, nw_ref, xsn_ref)
    else:
        xn_ref, xsn_ref = x_ref, xs_ref
    w = w_ref[...].astype(BF16)

    def contract(lhs_ref):
        return jnp.dot(lhs_ref[...], w, preferred_element_type=F32)

    def finish(acc, res_ref, out_ref):
        if has_bias:
            acc = acc + b_ref[...]
        if has_res:
            acc = acc + res_ref[...]
        out_ref[...] = acc.astype(out_ref.dtype)

    finish(contract(xn_ref), r_ref, o_ref)

    @pl.when(i == 0)
    def _():
        finish(contract(xsn_ref), rs_ref, os_ref)


def matmul(x, xs, w, layer, norm_w=None, bias=None, res=None, res_s=None, out_dtype=F32):
    m, k = x.shape
    ms = xs.shape[0]
    n = w.shape[2]
    tm = _tile(m, 1024)
    tn = _tile(n, 512)
    nj = n // tn
    js = lambda i, j: jnp.where(i == 0, j, nj - 1)
    has_norm = norm_w is not None
    once = pl.Buffered(1)
    in_specs = [
        pl.BlockSpec((tm, k), lambda i, j: (i, 0), **({"pipeline_mode": once} if has_norm else {})),
        pl.BlockSpec((ms, k), lambda i, j: (0, 0)),
        pl.BlockSpec((None, k, tn), lambda i, j: (layer, 0, j)),
    ]
    args = [x, xs, w]
    scratch = []
    if has_norm:
        in_specs.append(pl.BlockSpec((1, k), lambda i, j: (0, 0)))
        args.append(norm_w.reshape(1, k))
        scratch = [pltpu.VMEM((tm, k), BF16), pltpu.VMEM((ms, k), BF16)]
    if bias is not None:
        in_specs.append(pl.BlockSpec((1, tn), lambda i, j: (0, j)))
        args.append(bias.reshape(1, n))
    if res is not None:
        in_specs.append(pl.BlockSpec((tm, tn), lambda i, j: (i, j)))
        in_specs.append(pl.BlockSpec((ms, tn), lambda i, j: (0, js(i, j))))
        args += [res, res_s]
    return pl.pallas_call(
        functools.partial(_matmul_kernel, has_norm=has_norm, has_bias=bias is not None, has_res=res is not None),
        out_shape=(jax.ShapeDtypeStruct((m, n), out_dtype), jax.ShapeDtypeStruct((ms, n), out_dtype)),
        grid=(m // tm, nj),
        in_specs=in_specs,
        out_specs=(pl.BlockSpec((tm, tn), lambda i, j: (i, j)), pl.BlockSpec((ms, tn), lambda i, j: (0, js(i, j)))),
        scratch_shapes=scratch,
        compiler_params=_params("arbitrary", "arbitrary"),
        name="matmul",
    )(*args)


FFN_ROWS = 1024
FFN_OUT_COLS = 512
V7X_FFN_VMEM_LIMIT_BYTES = 60 * 1024 * 1024


def _final_norm_in_place(o_ref, fw_ref):
    total = o_ref.shape[0]
    rows = min(NORM_ROWS, total)

    def body(r, carry):
        rs = pl.ds(pl.multiple_of(r * rows, rows), rows)
        o_ref[rs, :] = _rmsnorm_rows(o_ref[rs, :], fw_ref[...])
        return carry

    lax.fori_loop(0, total // rows, body, 0, unroll=min(NORM_UNROLL, total // rows))


def _ffn_kernel(*refs, final_norm):
    h_hbm, hs_ref, nw_ref = refs[0], refs[1], refs[2]
    fw_ref = refs[3] if final_norm else None
    wg_ref, wu_ref, wo_ref, o_ref, os_ref, xn_ref, h_sem = refs[-7:]
    i, f = pl.program_id(0), pl.program_id(1)
    tm, d = o_ref.shape
    ms = os_ref.shape[0]
    last = pl.num_programs(1) - 1

    @pl.when(f == 0)
    def _():
        h_copy = pltpu.make_async_copy(h_hbm.at[pl.ds(pl.multiple_of(i * tm, tm), tm), :], o_ref, h_sem)
        h_copy.start()
        h_copy.wait()
        _norm_rows_into(o_ref, nw_ref, xn_ref.at[0:tm])

    @pl.when((f == 0) & (i == 0))
    def _():
        _norm_rows_into(hs_ref, nw_ref, xn_ref.at[tm:tm + ms], copy_ref=os_ref)

    def step(rows):
        xn = xn_ref[0:rows, :]
        g = jnp.dot(xn, wg_ref[...].astype(BF16), preferred_element_type=F32)
        u = jnp.dot(xn, wu_ref[...].astype(BF16), preferred_element_type=F32)
        act = ((g * jax.nn.sigmoid(g)) * u).astype(BF16)
        for c in range(0, d, FFN_OUT_COLS):
            cs = slice(c, min(c + FFN_OUT_COLS, d))
            r = jnp.dot(act, wo_ref[:, cs].astype(BF16), preferred_element_type=F32)
            o_ref[:, cs] += r[0:tm]
            if rows > tm:
                os_ref[:, cs] += r[tm:rows]

    @pl.when(i == 0)
    def _():
        step(tm + ms)

    @pl.when(i != 0)
    def _():
        step(tm)

    if final_norm:
        @pl.when(f == last)
        def _():
            _final_norm_in_place(o_ref, fw_ref)

        @pl.when((f == last) & (i == 0))
        def _():
            _final_norm_in_place(os_ref, fw_ref)


def ffn(h, hs, norm_w, w_in, w_out, layer, final_w=None):
    m, d = h.shape
    ms = hs.shape[0]
    d_ff = w_out.shape[1]
    tm = _tile(m, FFN_ROWS)
    tf = _tile(d_ff, 256)
    nf = d_ff // tf
    assert tm % BF16_SUBLANES == 0 and ms % BF16_SUBLANES == 0
    once = pl.Buffered(1)
    vec = pl.BlockSpec((1, d), lambda i, f: (0, 0), pipeline_mode=once)
    small = pl.BlockSpec((ms, d), lambda i, f: (0, 0), pipeline_mode=once)
    final_norm = final_w is not None
    return pl.pallas_call(
        functools.partial(_ffn_kernel, final_norm=final_norm),
        out_shape=(jax.ShapeDtypeStruct((m, d), F32), jax.ShapeDtypeStruct((ms, d), F32)),
        grid=(m // tm, nf),
        in_specs=[pl.BlockSpec(memory_space=pl.ANY), small, vec] + [vec] * final_norm + [
            pl.BlockSpec((None, d, tf), lambda i, f: (layer, 0, f)),
            pl.BlockSpec((None, d, tf), lambda i, f: (layer, 0, f + nf)),
            pl.BlockSpec((None, tf, d), lambda i, f: (layer, f, 0)),
        ],
        out_specs=(pl.BlockSpec((tm, d), lambda i, f: (i, 0), pipeline_mode=once),
                   pl.BlockSpec((ms, d), lambda i, f: (0, 0))),
        scratch_shapes=[pltpu.VMEM((tm + ms, d), BF16), pltpu.SemaphoreType.DMA(())],
        compiler_params=pltpu.CompilerParams(
            dimension_semantics=("arbitrary", "arbitrary"), vmem_limit_bytes=V7X_FFN_VMEM_LIMIT_BYTES),
        name="ffn",
    )(h, hs, norm_w.reshape(1, d), *([final_w.reshape(1, d)] if final_norm else []), w_in, w_in, w_out)


def _attn_prompt_kernel(sink_ref, q_ref, kc_ref, kp_ref, vc_ref, vp_ref, o_ref, *, n_kv, q_per_kv, hd, window):
    nb = pl.program_id(1)
    w = window
    assert 2 * hd == LANES and q_per_kv % 2 == 0 and n_kv % 2 == 0
    npair = q_per_kv // 2
    cols = npair * w
    q_low = lax.broadcasted_iota(jnp.int32, (w, LANES), 1) < hd
    kv_low = lax.broadcasted_iota(jnp.int32, (2 * w, LANES), 1) < hd
    kv_row0 = lax.broadcasted_iota(jnp.int32, (2 * w, LANES), 0) == 0
    kj = lax.broadcasted_iota(jnp.int32, (w, cols), 0)
    qi = lax.broadcasted_iota(jnp.int32, (w, cols), 1) & (w - 1)
    from_prev = kj > qi
    top_row0 = lax.broadcasted_iota(jnp.int32, (SUBLANES, cols), 0) == 0
    no_prev = jnp.where(nb > 0, 0.0, -jnp.inf)
    q_scale = hd ** -0.5 * LOG2E
    for h in range(n_kv):
        slab, half = divmod(h, 2)
        ksl = slice(slab * LANES, (slab + 1) * LANES)
        kk = [jnp.concatenate([kp_ref[:, ksl], kc_ref[:, ksl]], axis=0)]
        vv = [jnp.concatenate([vp_ref[:, ksl], vc_ref[:, ksl]], axis=0)]
        kk.append(pltpu.roll(kk[0], hd, axis=1))
        vv.append(pltpu.roll(vv[0], hd, axis=1))
        res = []
        for par in range(2):
            keep = q_low if par == 0 else ~q_low
            q = jnp.concatenate(
                [jnp.where(keep, q_ref[:, (h * npair + p) * LANES:(h * npair + p + 1) * LANES] * q_scale, 0.0)
                 for p in range(npair)], axis=0).astype(BF16)
            swap = int(par != half)
            s = lax.dot_general(kk[swap].astype(BF16), q, (((1,), (1,)), ((), ())), preferred_element_type=F32)
            s = jnp.where(from_prev, s[:w] + no_prev, s[w:])
            sink = jnp.concatenate(
                [jnp.full((1, w), sink_ref[h * q_per_kv + 2 * p + par] * LOG2E, F32) for p in range(npair)], axis=1)
            m = jnp.maximum(jnp.max(s, axis=0, keepdims=True), sink)
            e = jnp.exp2(s - m)
            e_prev = jnp.where(from_prev, e, 0.0)
            e_top = jnp.where(top_row0, jnp.exp2(sink - m), e_prev[:SUBLANES])
            e2 = jnp.concatenate([e_top, e_prev[SUBLANES:], jnp.where(from_prev, 0.0, e)], axis=0).astype(BF16)
            own = kv_low if par == 0 else ~kv_low
            v_aug = jnp.where(own, jnp.where(kv_row0, 0.0, vv[swap]), 1.0).astype(BF16)
            res.append(lax.dot_general(e2, v_aug, (((0,), (0,)), ((), ())), preferred_element_type=F32))
        for p in range(npair):
            r0, r1 = res[0][p * w:(p + 1) * w], res[1][p * w:(p + 1) * w]
            num = jnp.where(q_low, r0, r1)
            den = pltpu.roll(jnp.where(q_low, r1, r0), hd, axis=1)
            c = (h * npair + p) * LANES
            o_ref[:, c:c + LANES] = (num * (1.0 / den)).astype(o_ref.dtype)


def attn_prompt(qkv, sinks, batch, seq, n_kv, q_per_kv, hd, window):
    q_dim = n_kv * q_per_kv * hd
    kv_dim = n_kv * hd
    nblk = seq // window
    kcol = q_dim // kv_dim
    cur = lambda b, n: b * nblk + n
    prev = lambda b, n: b * nblk + jnp.maximum(n - 1, 0)
    return pl.pallas_call(
        functools.partial(_attn_prompt_kernel, n_kv=n_kv, q_per_kv=q_per_kv, hd=hd, window=window),
        out_shape=jax.ShapeDtypeStruct((batch * seq, q_dim), BF16),
        grid=(batch, nblk),
        in_specs=[
            pl.BlockSpec(memory_space=pltpu.SMEM),
            pl.BlockSpec((window, q_dim), lambda b, n: (cur(b, n), 0)),
            pl.BlockSpec((window, kv_dim), lambda b, n: (cur(b, n), kcol)),
            pl.BlockSpec((window, kv_dim), lambda b, n: (prev(b, n), kcol)),
            pl.BlockSpec((window, kv_dim), lambda b, n: (cur(b, n), kcol + 1)),
            pl.BlockSpec((window, kv_dim), lambda b, n: (prev(b, n), kcol + 1)),
        ],
        out_specs=pl.BlockSpec((window, q_dim), lambda b, n: (cur(b, n), 0)),
        compiler_params=_params("parallel", "arbitrary"),
        name="attn_prompt",
    )(sinks, qkv, qkv, qkv, qkv, qkv)


def _attn_sample_kernel(sink_ref, q_ref, kn_ref, vn_ref, ck_ref, cv_ref, o_ref, wk_ref, wv_ref, *, n_kv, q_per_kv, hd,
                        window):
    w = window
    wk_ref[0, 0:w - 1, :] = ck_ref[0, 1:w, :]
    wk_ref[0, w - 1:w, :] = kn_ref[0]
    wv_ref[0, 0:w - 1, :] = cv_ref[0, 1:w, :]
    wv_ref[0, w - 1:w, :] = vn_ref[0]
    scale = hd ** -0.5
    for h in range(n_kv):
        rs = slice(h * q_per_kv, (h + 1) * q_per_kv)
        ks = slice(h * hd, (h + 1) * hd)
        q = (q_ref[0, rs, :] * scale).astype(BF16)
        kk = wk_ref[0, :, ks].astype(BF16)
        vv = wv_ref[0, :, ks].astype(BF16)
        s = lax.dot_general(q, kk, (((1,), (1,)), ((), ())), preferred_element_type=F32)
        sink = sink_ref[rs, :]
        m = jnp.maximum(jnp.max(s, axis=-1, keepdims=True), sink)
        e = jnp.exp(s - m)
        den = jnp.sum(e, axis=-1, keepdims=True) + jnp.exp(sink - m)
        o_ref[0, rs, :] = jnp.dot(e.astype(BF16), vv, preferred_element_type=F32) * (1.0 / den)


def attn_sample(q, k_new, v_new, cache_k, cache_v, sinks, n_kv, q_per_kv, hd, window):
    nb = q.shape[0]
    nq = n_kv * q_per_kv
    kv_dim = n_kv * hd
    win = pl.BlockSpec((1, window, kv_dim), lambda i: (i, 0, 0))
    row = pl.BlockSpec((1, 1, kv_dim), lambda i: (i, 0, 0))
    qs = pl.BlockSpec((1, nq, hd), lambda i: (i, 0, 0))
    return pl.pallas_call(
        functools.partial(_attn_sample_kernel, n_kv=n_kv, q_per_kv=q_per_kv, hd=hd, window=window),
        out_shape=(
            jax.ShapeDtypeStruct((nb, nq, hd), F32),
            jax.ShapeDtypeStruct((nb, window, kv_dim), F32),
            jax.ShapeDtypeStruct((nb, window, kv_dim), F32),
        ),
        grid=(nb,),
        in_specs=[pl.BlockSpec((nq, 1), lambda i: (0, 0)), qs, row, row, win, win],
        out_specs=(qs, win, win),
        compiler_params=_params("parallel"),
        name="attn_sample",
    )(sinks.reshape(nq, 1), q, k_new, v_new, cache_k, cache_v)


def _lower_bound(l, layer):
    e = jnp.exp(l - jnp.max(l, axis=0, keepdims=True))
    p = e / jnp.sum(e, axis=0, keepdims=True)
    return jnp.sum(p[0:layer + 1], axis=0, keepdims=True) - p[0:1]


def _cumsum_rows(x, tri):
    k = x.shape[1]
    hi = x.astype(BF16)
    r1 = x - hi.astype(F32)
    mid = r1.astype(BF16)
    lo = (r1 - mid.astype(F32)).astype(BF16)
    parts = jnp.dot(tri, jnp.concatenate([hi, mid, lo], axis=1), preferred_element_type=F32)
    return (parts[:, 0:k] + parts[:, k:2 * k]) + parts[:, 2 * k:3 * k]


def _block_ref_rows(b, half):
    c, k = b.shape
    blk = 2 * half
    if blk >= 8:
        b3 = b.reshape(c // blk, blk, k)
        return jnp.broadcast_to(b3[:, half - 1:half, :], (c // blk, blk, k)).reshape(c, k)
    b3 = b.reshape(c // 8, 8, k)
    sub = lax.broadcasted_iota(jnp.int32, (c // 8, 8, k), 1)
    out = None
    for j in range(8 // blk):
        r = j * blk + half - 1
        cand = jnp.broadcast_to(b3[:, r:r + 1, :], (c // 8, 8, k))
        out = cand if out is None else jnp.where(sub >= j * blk, cand, out)
    return out.reshape(c, k)


def _gla_prompt_kernel(lb_ref, nw_ref, q_ref, f_ref, i_ref, g_ref, og_ref, st_ref, s_scr, *, layer, n_heads, kd, vd):
    c = q_ref.shape[0]
    ci = pl.program_id(1)
    n_chunks = pl.num_programs(1)

    @pl.when(ci == 0)
    def _():
        s_scr[...] = jnp.zeros_like(s_scr)

    nw = nw_ref[...]
    ti = lax.broadcasted_iota(jnp.int32, (c, c), 0)
    si = lax.broadcasted_iota(jnp.int32, (c, c), 1)
    pair_bits = jnp.where(ti > si, ti ^ si, 0)
    tri = (ti >= si).astype(BF16)
    lane = lax.broadcasted_iota(jnp.int32, (SUBLANES, c), 1)
    q_scale = kd ** -0.5
    zero_rows = {hf: jnp.zeros((hf, kd), F32) for hf in (SUBLANES << j for j in range(8)) if hf < c}

    def prep(h):
        ks = pl.ds(pl.multiple_of(h * kd, kd), kd)
        vs = pl.ds(pl.multiple_of(h * vd, vd), vd)
        lb = _lower_bound(lb_ref[:, ks], layer)
        qr = q_ref[:, ks]
        q = (qr * jax.nn.sigmoid(qr)) * q_scale
        forget = lb + (1.0 - lb) * jax.nn.sigmoid(f_ref[:, ks])
        v = i_ref[:, vs]
        b2 = _cumsum_rows(jnp.log2(forget), tri)
        return dict(h=h, vs=vs, q=q, k=1.0 - forget, forget=forget, v=v, v16=v.astype(BF16), b2=b2)

    def carry_state(t):
        q, k, b2, v16, h = t["q"], t["k"], t["b2"], t["v16"], t["h"]
        b_last = b2[c - 1:c, :]
        s_t = s_scr[h]
        t["o"] = lax.dot_general((q * jnp.exp2(b2)).astype(BF16), s_t.astype(BF16), (((1,), (1,)), ((), ())),
                                 preferred_element_type=F32)
        kdec = (k * jnp.exp2(b_last - b2)).astype(BF16)
        s_scr[h] = jnp.exp2(b_last) * s_t + lax.dot_general(v16, kdec, (((0,), (0,)), ((), ())),
                                                            preferred_element_type=F32)
        t["a"] = jnp.zeros((c, c), F32)

    def small_level(t, hf):
        q, k, b2 = t["q"], t["k"], t["b2"]
        if hf == 1:
            qe, ke = q * t["forget"], k
        else:
            e = jnp.exp2(-jnp.abs(b2 - _block_ref_rows(b2, hf)))
            qe, ke = q * e, k * e
        al = lax.dot_general(qe.astype(BF16), ke.astype(BF16), (((1,), (1,)), ((), ())),
                             preferred_element_type=F32)
        t["a"] = jnp.where(pair_bits >= hf, al, t["a"])

    def big_level(t, hf):
        q, k, b2, a_rows = t["q"], t["k"], t["b2"], t["a_rows"]
        q_parts, k_parts, uppers = [], [], []
        for blk in range(0, c, 2 * hf):
            lo, up = slice(blk, blk + hf), slice(blk + hf, blk + 2 * hf)
            r = b2[blk + hf - 1:blk + hf, :]
            q_parts.append(q[up] * jnp.exp2(b2[up] - r))
            k_parts += [k[lo] * jnp.exp2(r - b2[lo]), zero_rows[hf]]
            uppers += [(p, blk) for p in range((blk + hf) // SUBLANES, (blk + 2 * hf) // SUBLANES)]
        al = lax.dot_general(jnp.concatenate(q_parts, axis=0).astype(BF16),
                             jnp.concatenate(k_parts, axis=0).astype(BF16),
                             (((1,), (1,)), ((), ())), preferred_element_type=F32)
        for j, (p, blk) in enumerate(uppers):
            same_block = (lane >= blk) & (lane < blk + hf)
            a_rows[p] = jnp.where(same_block, al[j * SUBLANES:(j + 1) * SUBLANES], a_rows[p])

    def finish(t):
        q, k, v = t["q"], t["k"], t["v"]
        o = t["o"] + jnp.dot(jnp.concatenate(t["a_rows"], axis=0).astype(BF16), t["v16"], preferred_element_type=F32)
        o = o + jnp.sum(q * k, axis=-1, keepdims=True) * v
        o = (o * lax.rsqrt(jnp.mean(o * o, axis=-1, keepdims=True) + RMS_EPS)) * nw
        gr = g_ref[:, t["vs"]]
        og_ref[:, t["vs"]] = (o * (gr * jax.nn.sigmoid(gr))).astype(og_ref.dtype)

    group = min(GLA_HEAD_GROUP, n_heads)

    def head_group(hg, carry):
        ts = [prep(hg * group + j) for j in range(group)]
        for t in ts:
            carry_state(t)
        hf = 1
        while hf < min(SUBLANES, c):
            for t in ts:
                small_level(t, hf)
            hf *= 2
        for t in ts:
            t["a_rows"] = [t["a"][r:r + SUBLANES] for r in range(0, c, SUBLANES)]
        while hf < c:
            for t in ts:
                big_level(t, hf)
            hf *= 2
        for t in ts:
            finish(t)
        return carry

    lax.fori_loop(0, n_heads // group, head_group, 0)

    @pl.when(ci == n_chunks - 1)
    def _():
        def emit(h, carry):
            st_ref[0, h] = s_scr[h].T
            return carry

        lax.fori_loop(0, n_heads, emit, 0)


def gla_prompt(u, lower_bounds, norm_w, layer, batch, seq, n_heads, kd, vd):
    d = n_heads * kd
    c = _tile(seq, GLA_CHUNK)
    nch = seq // c
    n_layers = lower_bounds.shape[0]
    col = lambda j: pl.BlockSpec((c, d), lambda b, i: (b * nch + i, j))
    return pl.pallas_call(
        functools.partial(_gla_prompt_kernel, layer=layer, n_heads=n_heads, kd=kd, vd=vd),
        out_shape=(
            jax.ShapeDtypeStruct((batch * seq, d), BF16),
            jax.ShapeDtypeStruct((batch, n_heads, kd, vd), F32),
        ),
        grid=(batch, nch),
        in_specs=[
            pl.BlockSpec((n_layers, d), lambda b, i: (0, 0)),
            pl.BlockSpec((1, vd), lambda b, i: (0, 0)),
            col(0), col(1), col(2), col(3),
        ],
        out_specs=(
            pl.BlockSpec((c, d), lambda b, i: (b * nch + i, 0)),
            pl.BlockSpec((1, n_heads, kd, vd), lambda b, i: (b, 0, 0, 0)),
        ),
        scratch_shapes=[pltpu.VMEM((n_heads, vd, kd), F32)],
        compiler_params=_params("parallel", "arbitrary"),
        name="gla_prompt",
    )(lower_bounds, norm_w.reshape(1, vd), u, u, u, u)


def _gla_sample_kernel(lb_ref, nw_ref, x_ref, s_ref, og_ref, so_ref, *, layer, n_heads, kd):
    nh = n_heads
    x = x_ref[0]
    lb = _lower_bound(lb_ref[...], layer)[0]
    qr = x[0:nh]
    q = (qr * jax.nn.sigmoid(qr)) * (kd ** -0.5)
    forget = lb + (1.0 - lb) * jax.nn.sigmoid(x[nh:2 * nh])
    k = 1.0 - forget
    v = x[2 * nh:3 * nh]
    gr = x[3 * nh:4 * nh]
    pad = jnp.zeros((x.shape[0] - 3 * nh, kd), F32)
    cols = jnp.concatenate([q, forget, k, pad], axis=0).T
    outs = []
    for h in range(nh):
        s_new = cols[:, nh + h:nh + h + 1] * s_ref[0, h] + cols[:, 2 * nh + h:2 * nh + h + 1] * v[h:h + 1, :]
        so_ref[0, h] = s_new
        outs.append(jnp.sum(cols[:, h:h + 1] * s_new, axis=0, keepdims=True))
    o = jnp.concatenate(outs, axis=0)
    o = (o * lax.rsqrt(jnp.mean(o * o, axis=-1, keepdims=True) + RMS_EPS)) * nw_ref[...]
    og_ref[0] = o * (gr * jax.nn.sigmoid(gr))


def gla_sample(u, state, lower_bounds, norm_w, layer, n_heads, kd, vd):
    nb = u.shape[0]
    n_layers = lower_bounds.shape[0]
    rows = 4 * n_heads
    return pl.pallas_call(
        functools.partial(_gla_sample_kernel, layer=layer, n_heads=n_heads, kd=kd),
        out_shape=(
            jax.ShapeDtypeStruct((nb, n_heads, vd), F32),
            jax.ShapeDtypeStruct((nb, n_heads, kd, vd), F32),
        ),
        grid=(nb,),
        in_specs=[
            pl.BlockSpec((n_layers, n_heads, kd), lambda i: (0, 0, 0)),
            pl.BlockSpec((1, vd), lambda i: (0, 0)),
            pl.BlockSpec((1, rows, kd), lambda i: (i, 0, 0)),
            pl.BlockSpec((1, n_heads, kd, vd), lambda i: (i, 0, 0, 0)),
        ],
        out_specs=(
            pl.BlockSpec((1, n_heads, vd), lambda i: (i, 0, 0)),
            pl.BlockSpec((1, n_heads, kd, vd), lambda i: (i, 0, 0, 0)),
        ),
        compiler_params=_params("parallel"),
        name="gla_sample",
    )(lower_bounds.reshape(n_layers, n_heads, kd), norm_w.reshape(1, vd), u.reshape(nb, rows, kd), state)


def kernel(x_prompt, x_sample, cache_k, cache_v, state_hgrn, norm_mix, norm_ffn, norm_final, attn_w_qkv, attn_b_qkv,
           attn_sinks, attn_w_o, attn_b_o, hgrn_w_in, hgrn_lower_bounds, hgrn_norm, hgrn_w_o, ffn_w_in, ffn_w_out):
    batch, seq, d = x_prompt.shape
    nb = x_sample.shape[0]
    assert x_sample.shape[1] == 1
    depth = norm_mix.shape[0]
    window, n_kv, hd = cache_k.shape[2:]
    n_q = attn_sinks.shape[1]
    q_per_kv = n_q // n_kv
    q_dim, kv_dim = n_q * hd, n_kv * hd
    n_heads, kd, vd = state_hgrn.shape[2:]

    hp = x_prompt.reshape(batch * seq, d)
    hs = x_sample.reshape(nb, d)
    wk_p, wv_p, st_p, wk_s, wv_s, st_s = [], [], [], [], [], []
    for i in range(depth):
        if i % 2 == 0:
            a = i // 2
            qkv_p, qkv_s = matmul(hp, hs, attn_w_qkv, a, norm_w=norm_mix[i], bias=attn_b_qkv[a])
            op = attn_prompt(qkv_p, attn_sinks[a], batch, seq, n_kv, q_per_kv, hd, window)
            kv_p = qkv_p.reshape(batch, seq, q_dim + 2 * kv_dim)[:, seq - window:, q_dim:]
            wk_p.append(kv_p[..., :kv_dim].reshape(batch, window, n_kv, hd))
            wv_p.append(kv_p[..., kv_dim:].reshape(batch, window, n_kv, hd))
            os_, wk, wv = attn_sample(
                qkv_s[:, :q_dim].reshape(nb, n_q, hd),
                qkv_s[:, q_dim:q_dim + kv_dim].reshape(nb, 1, kv_dim),
                qkv_s[:, q_dim + kv_dim:].reshape(nb, 1, kv_dim),
                cache_k[a].reshape(nb, window, kv_dim), cache_v[a].reshape(nb, window, kv_dim),
                attn_sinks[a], n_kv, q_per_kv, hd, window)
            wk_s.append(wk.reshape(nb, window, n_kv, hd))
            wv_s.append(wv.reshape(nb, window, n_kv, hd))
            hp, hs = matmul(op, os_.reshape(nb, q_dim).astype(BF16), attn_w_o, a, bias=attn_b_o[a], res=hp, res_s=hs)
        else:
            r = i // 2
            up, us = matmul(hp, hs, hgrn_w_in, r, norm_w=norm_mix[i])
            ogp, sp = gla_prompt(up, hgrn_lower_bounds, hgrn_norm[r], i, batch, seq, n_heads, kd, vd)
            ogs, ss = gla_sample(us, state_hgrn[r], hgrn_lower_bounds, hgrn_norm[r], i, n_heads, kd, vd)
            st_p.append(sp)
            st_s.append(ss)
            hp, hs = matmul(ogp, ogs.reshape(nb, d).astype(BF16), hgrn_w_o, r, res=hp, res_s=hs)
        final_w = norm_final if i == depth - 1 else None
        hp, hs = ffn(hp, hs, norm_ffn[i], ffn_w_in, ffn_w_out, i, final_w=final_w)

    y_prompt = hp.reshape(batch, seq, d)
    y_sample = hs.reshape(nb, 1, d)
    return (y_prompt, y_sample, jnp.stack(wk_p), jnp.stack(wv_p), jnp.stack(st_p), jnp.stack(wk_s), jnp.stack(wv_s),
            jnp.stack(st_s))
```

```python
import functools

import jax
import jax.numpy as jnp
from jax import lax
from jax.experimental import pallas as pl
from jax.experimental.pallas import tpu as pltpu

RMS_EPS = 1e-5
BF16 = jnp.bfloat16
F32 = jnp.float32

V7X_VMEM_LIMIT_BYTES = 56 * 1024 * 1024
V7X_WIDE_VMEM_LIMIT_BYTES = 60 * 1024 * 1024
GLA_CHUNK = 128
GLA_HEAD_GROUP = 8
SUBLANES = 8
BF16_SUBLANES = 16
LANES = 128
LOG2E = 1.4426950408889634


def _params(*sem):
    return pltpu.CompilerParams(dimension_semantics=sem, vmem_limit_bytes=V7X_VMEM_LIMIT_BYTES)


def _tile(n, pref):
    return pref if n % pref == 0 else n


def _rmsnorm_rows(x, w):
    ms = jnp.mean(x * x, axis=-1, keepdims=True)
    return (x * lax.rsqrt(ms + RMS_EPS)) * w


NORM_ROWS = 16
NORM_UNROLL = 4


def _norm_rows_into(src_ref, nw_ref, dst_ref, copy_ref=None):
    total = src_ref.shape[0]
    rows = min(NORM_ROWS, total)

    def body(r, carry):
        rs = pl.ds(pl.multiple_of(r * rows, rows), rows)
        h = src_ref[rs, :]
        dst_ref[rs, :] = _rmsnorm_rows(h, nw_ref[...]).astype(BF16)
        if copy_ref is not None:
            copy_ref[rs, :] = h
        return carry

    lax.fori_loop(0, total // rows, body, 0, unroll=min(NORM_UNROLL, total // rows))


TILE_LOAD_CHUNKS = 8


def _load_chunks(total):
    return TILE_LOAD_CHUNKS if total % (TILE_LOAD_CHUNKS * NORM_ROWS) == 0 else 1


def _matmul_kernel(*refs, has_norm, has_bias, has_res):
    x_ref, xs_ref, w_ref = refs[0], refs[1], refs[2]
    pos = 3
    nw_ref = b_ref = r_ref = rs_ref = None
    if has_norm:
        nw_ref = refs[pos]
        pos += 1
    if has_bias:
        b_ref = refs[pos]
        pos += 1
    if has_res:
        r_ref, rs_ref = refs[pos], refs[pos + 1]
        pos += 2
    o_ref, os_ref = refs[pos], refs[pos + 1]
    i, j = pl.program_id(0), pl.program_id(1)
    if has_norm:
        xn_ref, xsn_ref, stage_ref, x_sem = refs[pos + 2:pos + 6]
        tm = xn_ref.shape[0]

        @pl.when(j == 0)
        def _():
            n = _load_chunks(tm)
            rows = tm // n

            def piece(c):
                return pltpu.make_async_copy(
                    x_ref.at[pl.ds(pl.multiple_of(i * tm + c * rows, rows), rows), :], stage_ref.at[c % 2],
                    x_sem.at[c % 2])

            piece(0).start()
            for c in range(n):
                if c + 1 < n:
                    piece(c + 1).start()
                piece(c).wait()
                _norm_rows_into(stage_ref.at[c % 2], nw_ref, xn_ref.at[c * rows:(c + 1) * rows])

        @pl.when((j == 0) & (i == 0))
        def _():
            _norm_rows_into(xs_ref, nw_ref, xsn_ref)
    else:
        xn_ref, xsn_ref = x_ref, xs_ref
    w = w_ref[...].astype(BF16)

    def contract(lhs_ref):
        return jnp.dot(lhs_ref[...], w, preferred_element_type=F32)

    def finish(acc, res_ref, out_ref):
        if has_bias:
            acc = acc + b_ref[...]
        if has_res:
            acc = acc + res_ref[...]
        out_ref[...] = acc.astype(out_ref.dtype)

    finish(contract(xn_ref), r_ref, o_ref)

    @pl.when(i == 0)
    def _():
        finish(contract(xsn_ref), rs_ref, os_ref)


def matmul(x, xs, w, layer, norm_w=None, bias=None, res=None, res_s=None, out_dtype=F32):
    m, k = x.shape
    ms = xs.shape[0]
    n = w.shape[2]
    has_norm = norm_w is not None
    tm = _tile(m, 1024)
    tn = _tile(n, 512)
    nj = n // tn
    js = lambda i, j: jnp.where(i == 0, j, nj - 1)
    in_specs = [
        pl.BlockSpec(memory_space=pl.ANY) if has_norm else pl.BlockSpec((tm, k), lambda i, j: (i, 0)),
        pl.BlockSpec((ms, k), lambda i, j: (0, 0)),
        pl.BlockSpec((None, k, tn), lambda i, j: (layer, 0, j)),
    ]
    args = [x, xs, w]
    scratch = []
    if has_norm:
        in_specs.append(pl.BlockSpec((1, k), lambda i, j: (0, 0)))
        args.append(norm_w.reshape(1, k))
        scratch = [pltpu.VMEM((tm, k), BF16), pltpu.VMEM((ms, k), BF16),
                   pltpu.VMEM((2, tm // _load_chunks(tm), k), F32), pltpu.SemaphoreType.DMA((2,))]
    if bias is not None:
        in_specs.append(pl.BlockSpec((1, tn), lambda i, j: (0, j)))
        args.append(bias.reshape(1, n))
    if res is not None:
        in_specs.append(pl.BlockSpec((tm, tn), lambda i, j: (i, j)))
        in_specs.append(pl.BlockSpec((ms, tn), lambda i, j: (0, js(i, j))))
        args += [res, res_s]
    return pl.pallas_call(
        functools.partial(_matmul_kernel, has_norm=has_norm, has_bias=bias is not None, has_res=res is not None),
        out_shape=(jax.ShapeDtypeStruct((m, n), out_dtype), jax.ShapeDtypeStruct((ms, n), out_dtype)),
        grid=(m // tm, nj),
        in_specs=in_specs,
        out_specs=(pl.BlockSpec((tm, tn), lambda i, j: (i, j)), pl.BlockSpec((ms, tn), lambda i, j: (0, js(i, j)))),
        scratch_shapes=scratch,
        compiler_params=_params("arbitrary", "arbitrary"),
        name="matmul",
    )(*args)


FFN_ROWS = 1024
FFN_OUT_COLS = 512


def _final_norm_in_place(o_ref, fw_ref):
    total = o_ref.shape[0]
    rows = min(NORM_ROWS, total)

    def body(r, carry):
        rs = pl.ds(pl.multiple_of(r * rows, rows), rows)
        o_ref[rs, :] = _rmsnorm_rows(o_ref[rs, :], fw_ref[...])
        return carry

    lax.fori_loop(0, total // rows, body, 0, unroll=min(NORM_UNROLL, total // rows))


def _ffn_kernel(*refs, final_norm):
    h_hbm, hs_ref, nw_ref = refs[0], refs[1], refs[2]
    fw_ref = refs[3] if final_norm else None
    wg_ref, wu_ref, wo_ref, o_ref, os_ref, xn_ref, h_sem = refs[-7:]
    i, f = pl.program_id(0), pl.program_id(1)
    tm, d = o_ref.shape
    ms = os_ref.shape[0]
    last = pl.num_programs(1) - 1

    @pl.when(f == 0)
    def _():
        n = _load_chunks(tm)
        rows = tm // n
        copies = [pltpu.make_async_copy(h_hbm.at[pl.ds(pl.multiple_of(i * tm + c * rows, rows), rows), :],
                                        o_ref.at[c * rows:(c + 1) * rows], h_sem.at[c]) for c in range(n)]
        for cp in copies:
            cp.start()
        for c, cp in enumerate(copies):
            cp.wait()
            _norm_rows_into(o_ref.at[c * rows:(c + 1) * rows], nw_ref, xn_ref.at[c * rows:(c + 1) * rows])

    @pl.when((f == 0) & (i == 0))
    def _():
        _norm_rows_into(hs_ref, nw_ref, xn_ref.at[tm:tm + ms], copy_ref=os_ref)

    def step(rows):
        xn = xn_ref[0:rows, :]
        g = jnp.dot(xn, wg_ref[...].astype(BF16), preferred_element_type=F32)
        u = jnp.dot(xn, wu_ref[...].astype(BF16), preferred_element_type=F32)
        act = ((g * jax.nn.sigmoid(g)) * u).astype(BF16)
        for c in range(0, d, FFN_OUT_COLS):
            cs = slice(c, min(c + FFN_OUT_COLS, d))
            r = jnp.dot(act, wo_ref[:, cs].astype(BF16), preferred_element_type=F32)
            o_ref[:, cs] += r[0:tm]
            if rows > tm:
                os_ref[:, cs] += r[tm:rows]

    @pl.when(i == 0)
    def _():
        step(tm + ms)

    @pl.when(i != 0)
    def _():
        step(tm)

    if final_norm:
        @pl.when(f == last)
        def _():
            _final_norm_in_place(o_ref, fw_ref)

        @pl.when((f == last) & (i == 0))
        def _():
            _final_norm_in_place(os_ref, fw_ref)


def ffn(h, hs, norm_w, w_in, w_out, layer, final_w=None):
    m, d = h.shape
    ms = hs.shape[0]
    d_ff = w_out.shape[1]
    tm = _tile(m, FFN_ROWS)
    tf = _tile(d_ff, 256)
    nf = d_ff // tf
    assert tm % BF16_SUBLANES == 0 and ms % BF16_SUBLANES == 0
    once = pl.Buffered(1)
    vec = pl.BlockSpec((1, d), lambda i, f: (0, 0), pipeline_mode=once)
    small = pl.BlockSpec((ms, d), lambda i, f: (0, 0), pipeline_mode=once)
    final_norm = final_w is not None
    return pl.pallas_call(
        functools.partial(_ffn_kernel, final_norm=final_norm),
        out_shape=(jax.ShapeDtypeStruct((m, d), F32), jax.ShapeDtypeStruct((ms, d), F32)),
        grid=(m // tm, nf),
        in_specs=[pl.BlockSpec(memory_space=pl.ANY), small, vec] + [vec] * final_norm + [
            pl.BlockSpec((None, d, tf), lambda i, f: (layer, 0, f)),
            pl.BlockSpec((None, d, tf), lambda i, f: (layer, 0, f + nf)),
            pl.BlockSpec((None, tf, d), lambda i, f: (layer, f, 0)),
        ],
        out_specs=(pl.BlockSpec((tm, d), lambda i, f: (i, 0), pipeline_mode=once),
                   pl.BlockSpec((ms, d), lambda i, f: (0, 0))),
        scratch_shapes=[pltpu.VMEM((tm + ms, d), BF16), pltpu.SemaphoreType.DMA((_load_chunks(tm),))],
        compiler_params=pltpu.CompilerParams(
            dimension_semantics=("arbitrary", "arbitrary"), vmem_limit_bytes=V7X_WIDE_VMEM_LIMIT_BYTES),
        name="ffn",
    )(h, hs, norm_w.reshape(1, d), *([final_w.reshape(1, d)] if final_norm else []), w_in, w_in, w_out)


def _attn_prompt_kernel(sink_ref, q_ref, kc_ref, kp_ref, vc_ref, vp_ref, o_ref, *, n_kv, q_per_kv, hd, window):
    nb = pl.program_id(1)
    w = window
    assert 2 * hd == LANES and q_per_kv % 2 == 0 and n_kv % 2 == 0
    npair = q_per_kv // 2
    cols = npair * w
    q_low = lax.broadcasted_iota(jnp.int32, (w, LANES), 1) < hd
    kv_low = lax.broadcasted_iota(jnp.int32, (2 * w, LANES), 1) < hd
    kv_row0 = lax.broadcasted_iota(jnp.int32, (2 * w, LANES), 0) == 0
    kj = lax.broadcasted_iota(jnp.int32, (w, cols), 0)
    qi = lax.broadcasted_iota(jnp.int32, (w, cols), 1) & (w - 1)
    from_prev = kj > qi
    top_row0 = lax.broadcasted_iota(jnp.int32, (SUBLANES, cols), 0) == 0
    no_prev = jnp.where(nb > 0, 0.0, -jnp.inf)
    q_scale = hd ** -0.5 * LOG2E
    for h in range(n_kv):
        slab, half = divmod(h, 2)
        ksl = slice(slab * LANES, (slab + 1) * LANES)
        kk = [jnp.concatenate([kp_ref[:, ksl], kc_ref[:, ksl]], axis=0)]
        vv = [jnp.concatenate([vp_ref[:, ksl], vc_ref[:, ksl]], axis=0)]
        kk.append(pltpu.roll(kk[0], hd, axis=1))
        vv.append(pltpu.roll(vv[0], hd, axis=1))
        res = []
        for par in range(2):
            keep = q_low if par == 0 else ~q_low
            q = jnp.concatenate(
                [jnp.where(keep, q_ref[:, (h * npair + p) * LANES:(h * npair + p + 1) * LANES] * q_scale, 0.0)
                 for p in range(npair)], axis=0).astype(BF16)
            swap = int(par != half)
            s = lax.dot_general(kk[swap].astype(BF16), q, (((1,), (1,)), ((), ())), preferred_element_type=F32)
            s = jnp.where(from_prev, s[:w] + no_prev, s[w:])
            sink = jnp.concatenate(
                [jnp.full((1, w), sink_ref[h * q_per_kv + 2 * p + par] * LOG2E, F32) for p in range(npair)], axis=1)
            m = jnp.maximum(jnp.max(s, axis=0, keepdims=True), sink)
            e = jnp.exp2(s - m)
            e_prev = jnp.where(from_prev, e, 0.0)
            e_top = jnp.where(top_row0, jnp.exp2(sink - m), e_prev[:SUBLANES])
            e2 = jnp.concatenate([e_top, e_prev[SUBLANES:], jnp.where(from_prev, 0.0, e)], axis=0).astype(BF16)
            own = kv_low if par == 0 else ~kv_low
            v_aug = jnp.where(own, jnp.where(kv_row0, 0.0, vv[swap]), 1.0).astype(BF16)
            res.append(lax.dot_general(e2, v_aug, (((0,), (0,)), ((), ())), preferred_element_type=F32))
        for p in range(npair):
            r0, r1 = res[0][p * w:(p + 1) * w], res[1][p * w:(p + 1) * w]
            num = jnp.where(q_low, r0, r1)
            den = pltpu.roll(jnp.where(q_low, r1, r0), hd, axis=1)
            c = (h * npair + p) * LANES
            o_ref[:, c:c + LANES] = (num * (1.0 / den)).astype(o_ref.dtype)


def attn_prompt(qkv, sinks, batch, seq, n_kv, q_per_kv, hd, window):
    q_dim = n_kv * q_per_kv * hd
    kv_dim = n_kv * hd
    nblk = seq // window
    kcol = q_dim // kv_dim
    cur = lambda b, n: b * nblk + n
    prev = lambda b, n: b * nblk + jnp.maximum(n - 1, 0)
    return pl.pallas_call(
        functools.partial(_attn_prompt_kernel, n_kv=n_kv, q_per_kv=q_per_kv, hd=hd, window=window),
        out_shape=jax.ShapeDtypeStruct((batch * seq, q_dim), BF16),
        grid=(batch, nblk),
        in_specs=[
            pl.BlockSpec(memory_space=pltpu.SMEM),
            pl.BlockSpec((window, q_dim), lambda b, n: (cur(b, n), 0)),
            pl.BlockSpec((window, kv_dim), lambda b, n: (cur(b, n), kcol)),
            pl.BlockSpec((window, kv_dim), lambda b, n: (prev(b, n), kcol)),
            pl.BlockSpec((window, kv_dim), lambda b, n: (cur(b, n), kcol + 1)),
            pl.BlockSpec((window, kv_dim), lambda b, n: (prev(b, n), kcol + 1)),
        ],
        out_specs=pl.BlockSpec((window, q_dim), lambda b, n: (cur(b, n), 0)),
        compiler_params=_params("parallel", "arbitrary"),
        name="attn_prompt",
    )(sinks, qkv, qkv, qkv, qkv, qkv)


def _attn_sample_kernel(sink_ref, q_ref, kn_ref, vn_ref, ck_ref, cv_ref, o_ref, wk_ref, wv_ref, *, n_kv, q_per_kv, hd,
                        window):
    w = window
    wk_ref[0, 0:w - 1, :] = ck_ref[0, 1:w, :]
    wk_ref[0, w - 1:w, :] = kn_ref[0]
    wv_ref[0, 0:w - 1, :] = cv_ref[0, 1:w, :]
    wv_ref[0, w - 1:w, :] = vn_ref[0]
    scale = hd ** -0.5
    for h in range(n_kv):
        rs = slice(h * q_per_kv, (h + 1) * q_per_kv)
        ks = slice(h * hd, (h + 1) * hd)
        q = (q_ref[0, rs, :] * scale).astype(BF16)
        kk = wk_ref[0, :, ks].astype(BF16)
        vv = wv_ref[0, :, ks].astype(BF16)
        s = lax.dot_general(q, kk, (((1,), (1,)), ((), ())), preferred_element_type=F32)
        sink = sink_ref[rs, :]
        m = jnp.maximum(jnp.max(s, axis=-1, keepdims=True), sink)
        e = jnp.exp(s - m)
        den = jnp.sum(e, axis=-1, keepdims=True) + jnp.exp(sink - m)
        o_ref[0, rs, :] = jnp.dot(e.astype(BF16), vv, preferred_element_type=F32) * (1.0 / den)


def attn_sample(q, k_new, v_new, cache_k, cache_v, sinks, n_kv, q_per_kv, hd, window):
    nb = q.shape[0]
    nq = n_kv * q_per_kv
    kv_dim = n_kv * hd
    win = pl.BlockSpec((1, window, kv_dim), lambda i: (i, 0, 0))
    row = pl.BlockSpec((1, 1, kv_dim), lambda i: (i, 0, 0))
    qs = pl.BlockSpec((1, nq, hd), lambda i: (i, 0, 0))
    return pl.pallas_call(
        functools.partial(_attn_sample_kernel, n_kv=n_kv, q_per_kv=q_per_kv, hd=hd, window=window),
        out_shape=(
            jax.ShapeDtypeStruct((nb, nq, hd), F32),
            jax.ShapeDtypeStruct((nb, window, kv_dim), F32),
            jax.ShapeDtypeStruct((nb, window, kv_dim), F32),
        ),
        grid=(nb,),
        in_specs=[pl.BlockSpec((nq, 1), lambda i: (0, 0)), qs, row, row, win, win],
        out_specs=(qs, win, win),
        compiler_params=_params("parallel"),
        name="attn_sample",
    )(sinks.reshape(nq, 1), q, k_new, v_new, cache_k, cache_v)


def _lower_bound(l, layer):
    e = jnp.exp(l - jnp.max(l, axis=0, keepdims=True))
    p = e / jnp.sum(e, axis=0, keepdims=True)
    return jnp.sum(p[0:layer + 1], axis=0, keepdims=True) - p[0:1]


def _cumsum_rows(x, tri):
    k = x.shape[1]
    hi = x.astype(BF16)
    r1 = x - hi.astype(F32)
    mid = r1.astype(BF16)
    lo = (r1 - mid.astype(F32)).astype(BF16)
    parts = jnp.dot(tri, jnp.concatenate([hi, mid, lo], axis=1), preferred_element_type=F32)
    return (parts[:, 0:k] + parts[:, k:2 * k]) + parts[:, 2 * k:3 * k]


def _block_ref_rows(b, half):
    c, k = b.shape
    blk = 2 * half
    if blk >= 8:
        b3 = b.reshape(c // blk, blk, k)
        return jnp.broadcast_to(b3[:, half - 1:half, :], (c // blk, blk, k)).reshape(c, k)
    b3 = b.reshape(c // 8, 8, k)
    sub = lax.broadcasted_iota(jnp.int32, (c // 8, 8, k), 1)
    out = None
    for j in range(8 // blk):
        r = j * blk + half - 1
        cand = jnp.broadcast_to(b3[:, r:r + 1, :], (c // 8, 8, k))
        out = cand if out is None else jnp.where(sub >= j * blk, cand, out)
    return out.reshape(c, k)


def _gla_prompt_kernel(lb_ref, nw_ref, q_ref, f_ref, i_ref, g_ref, og_ref, st_ref, s_scr, *, layer, n_heads, kd, vd):
    c = q_ref.shape[0]
    ci = pl.program_id(1)
    n_chunks = pl.num_programs(1)

    @pl.when(ci == 0)
    def _():
        s_scr[...] = jnp.zeros_like(s_scr)

    nw = nw_ref[...]
    ti = lax.broadcasted_iota(jnp.int32, (c, c), 0)
    si = lax.broadcasted_iota(jnp.int32, (c, c), 1)
    pair_bits = jnp.where(ti > si, ti ^ si, 0)
    tri = (ti >= si).astype(BF16)
    lane = lax.broadcasted_iota(jnp.int32, (SUBLANES, c), 1)
    q_scale = kd ** -0.5
    zero_rows = {hf: jnp.zeros((hf, kd), F32) for hf in (SUBLANES << j for j in range(8)) if hf < c}

    def prep(h):
        ks = pl.ds(pl.multiple_of(h * kd, kd), kd)
        vs = pl.ds(pl.multiple_of(h * vd, vd), vd)
        lb = _lower_bound(lb_ref[:, ks], layer)
        qr = q_ref[:, ks]
        q = (qr * jax.nn.sigmoid(qr)) * q_scale
        forget = lb + (1.0 - lb) * jax.nn.sigmoid(f_ref[:, ks])
        v = i_ref[:, vs]
        b2 = _cumsum_rows(jnp.log2(forget), tri)
        return dict(h=h, vs=vs, q=q, k=1.0 - forget, forget=forget, v=v, v16=v.astype(BF16), b2=b2)

    def carry_state(t):
        q, k, b2, v16, h = t["q"], t["k"], t["b2"], t["v16"], t["h"]
        b_last = b2[c - 1:c, :]
        s_t = s_scr[h]
        t["o"] = lax.dot_general((q * jnp.exp2(b2)).astype(BF16), s_t.astype(BF16), (((1,), (1,)), ((), ())),
                                 preferred_element_type=F32)
        kdec = (k * jnp.exp2(b_last - b2)).astype(BF16)
        s_scr[h] = jnp.exp2(b_last) * s_t + lax.dot_general(v16, kdec, (((0,), (0,)), ((), ())),
                                                            preferred_element_type=F32)
        t["a"] = jnp.zeros((c, c), F32)

    def small_level(t, hf):
        q, k, b2 = t["q"], t["k"], t["b2"]
        if hf == 1:
            qe, ke = q * t["forget"], k
        else:
            e = jnp.exp2(-jnp.abs(b2 - _block_ref_rows(b2, hf)))
            qe, ke = q * e, k * e
        al = lax.dot_general(qe.astype(BF16), ke.astype(BF16), (((1,), (1,)), ((), ())),
                             preferred_element_type=F32)
        t["a"] = jnp.where(pair_bits >= hf, al, t["a"])

    def big_level(t, hf):
        q, k, b2, a_rows = t["q"], t["k"], t["b2"], t["a_rows"]
        q_parts, k_parts, uppers = [], [], []
        for blk in range(0, c, 2 * hf):
            lo, up = slice(blk, blk + hf), slice(blk + hf, blk + 2 * hf)
            r = b2[blk + hf - 1:blk + hf, :]
            q_parts.append(q[up] * jnp.exp2(b2[up] - r))
            k_parts += [k[lo] * jnp.exp2(r - b2[lo]), zero_rows[hf]]
            uppers += [(p, blk) for p in range((blk + hf) // SUBLANES, (blk + 2 * hf) // SUBLANES)]
        al = lax.dot_general(jnp.concatenate(q_parts, axis=0).astype(BF16),
                             jnp.concatenate(k_parts, axis=0).astype(BF16),
                             (((1,), (1,)), ((), ())), preferred_element_type=F32)
        for j, (p, blk) in enumerate(uppers):
            same_block = (lane >= blk) & (lane < blk + hf)
            a_rows[p] = jnp.where(same_block, al[j * SUBLANES:(j + 1) * SUBLANES], a_rows[p])

    def finish(t):
        q, k, v = t["q"], t["k"], t["v"]
        o = t["o"] + jnp.dot(jnp.concatenate(t["a_rows"], axis=0).astype(BF16), t["v16"], preferred_element_type=F32)
        o = o + jnp.sum(q * k, axis=-1, keepdims=True) * v
        o = (o * lax.rsqrt(jnp.mean(o * o, axis=-1, keepdims=True) + RMS_EPS)) * nw
        gr = g_ref[:, t["vs"]]
        og_ref[:, t["vs"]] = (o * (gr * jax.nn.sigmoid(gr))).astype(og_ref.dtype)

    group = min(GLA_HEAD_GROUP, n_heads)

    def head_group(hg, carry):
        ts = [prep(hg * group + j) for j in range(group)]
        for t in ts:
            carry_state(t)
        hf = 1
        while hf < min(SUBLANES, c):
            for t in ts:
                small_level(t, hf)
            hf *= 2
        for t in ts:
            t["a_rows"] = [t["a"][r:r + SUBLANES] for r in range(0, c, SUBLANES)]
        while hf < c:
            for t in ts:
                big_level(t, hf)
            hf *= 2
        for t in ts:
            finish(t)
        return carry

    lax.fori_loop(0, n_heads // group, head_group, 0)

    @pl.when(ci == n_chunks - 1)
    def _():
        def emit(h, carry):
            st_ref[0, h] = s_scr[h].T
            return carry

        lax.fori_loop(0, n_heads, emit, 0)


def gla_prompt(u, lower_bounds, norm_w, layer, batch, seq, n_heads, kd, vd):
    d = n_heads * kd
    c = _tile(seq, GLA_CHUNK)
    nch = seq // c
    n_layers = lower_bounds.shape[0]
    col = lambda j: pl.BlockSpec((c, d), lambda b, i: (b * nch + i, j))
    return pl.pallas_call(
        functools.partial(_gla_prompt_kernel, layer=layer, n_heads=n_heads, kd=kd, vd=vd),
        out_shape=(
            jax.ShapeDtypeStruct((batch * seq, d), BF16),
            jax.ShapeDtypeStruct((batch, n_heads, kd, vd), F32),
        ),
        grid=(batch, nch),
        in_specs=[
            pl.BlockSpec((n_layers, d), lambda b, i: (0, 0)),
            pl.BlockSpec((1, vd), lambda b, i: (0, 0)),
            col(0), col(1), col(2), col(3),
        ],
        out_specs=(
            pl.BlockSpec((c, d), lambda b, i: (b * nch + i, 0)),
            pl.BlockSpec((1, n_heads, kd, vd), lambda b, i: (b, 0, 0, 0)),
        ),
        scratch_shapes=[pltpu.VMEM((n_heads, vd, kd), F32)],
        compiler_params=_params("parallel", "arbitrary"),
        name="gla_prompt",
    )(lower_bounds, norm_w.reshape(1, vd), u, u, u, u)


def _gla_sample_kernel(lb_ref, nw_ref, x_ref, s_ref, og_ref, so_ref, *, layer, n_heads, kd):
    nh = n_heads
    x = x_ref[0]
    lb = _lower_bound(lb_ref[...], layer)[0]
    qr = x[0:nh]
    q = (qr * jax.nn.sigmoid(qr)) * (kd ** -0.5)
    forget = lb + (1.0 - lb) * jax.nn.sigmoid(x[nh:2 * nh])
    k = 1.0 - forget
    v = x[2 * nh:3 * nh]
    gr = x[3 * nh:4 * nh]
    pad = jnp.zeros((x.shape[0] - 3 * nh, kd), F32)
    cols = jnp.concatenate([q, forget, k, pad], axis=0).T
    outs = []
    for h in range(nh):
        s_new = cols[:, nh + h:nh + h + 1] * s_ref[0, h] + cols[:, 2 * nh + h:2 * nh + h + 1] * v[h:h + 1, :]
        so_ref[0, h] = s_new
        outs.append(jnp.sum(cols[:, h:h + 1] * s_new, axis=0, keepdims=True))
    o = jnp.concatenate(outs, axis=0)
    o = (o * lax.rsqrt(jnp.mean(o * o, axis=-1, keepdims=True) + RMS_EPS)) * nw_ref[...]
    og_ref[0] = o * (gr * jax.nn.sigmoid(gr))


def gla_sample(u, state, lower_bounds, norm_w, layer, n_heads, kd, vd):
    nb = u.shape[0]
    n_layers = lower_bounds.shape[0]
    rows = 4 * n_heads
    return pl.pallas_call(
        functools.partial(_gla_sample_kernel, layer=layer, n_heads=n_heads, kd=kd),
        out_shape=(
            jax.ShapeDtypeStruct((nb, n_heads, vd), F32),
            jax.ShapeDtypeStruct((nb, n_heads, kd, vd), F32),
        ),
        grid=(nb,),
        in_specs=[
            pl.BlockSpec((n_layers, n_heads, kd), lambda i: (0, 0, 0)),
            pl.BlockSpec((1, vd), lambda i: (0, 0)),
            pl.BlockSpec((1, rows, kd), lambda i: (i, 0, 0)),
            pl.BlockSpec((1, n_heads, kd, vd), lambda i: (i, 0, 0, 0)),
        ],
        out_specs=(
            pl.BlockSpec((1, n_heads, vd), lambda i: (i, 0, 0)),
            pl.BlockSpec((1, n_heads, kd, vd), lambda i: (i, 0, 0, 0)),
        ),
        compiler_params=_params("parallel"),
        name="gla_sample",
    )(lower_bounds.reshape(n_layers, n_heads, kd), norm_w.reshape(1, vd), u.reshape(nb, rows, kd), state)


def kernel(x_prompt, x_sample, cache_k, cache_v, state_hgrn, norm_mix, norm_ffn, norm_final, attn_w_qkv, attn_b_qkv,
           attn_sinks, attn_w_o, attn_b_o, hgrn_w_in, hgrn_lower_bounds, hgrn_norm, hgrn_w_o, ffn_w_in, ffn_w_out):
    batch, seq, d = x_prompt.shape
    nb = x_sample.shape[0]
    assert x_sample.shape[1] == 1
    depth = norm_mix.shape[0]
    window, n_kv, hd = cache_k.shape[2:]
    n_q = attn_sinks.shape[1]
    q_per_kv = n_q // n_kv
    q_dim, kv_dim = n_q * hd, n_kv * hd
    n_heads, kd, vd = state_hgrn.shape[2:]

    hp = x_prompt.reshape(batch * seq, d)
    hs = x_sample.reshape(nb, d)
    wk_p, wv_p, st_p, wk_s, wv_s, st_s = [], [], [], [], [], []
    for i in range(depth):
        if i % 2 == 0:
            a = i // 2
            qkv_p, qkv_s = matmul(hp, hs, attn_w_qkv, a, norm_w=norm_mix[i], bias=attn_b_qkv[a])
            op = attn_prompt(qkv_p, attn_sinks[a], batch, seq, n_kv, q_per_kv, hd, window)
            kv_p = qkv_p.reshape(batch, seq, q_dim + 2 * kv_dim)[:, seq - window:, q_dim:]
            wk_p.append(kv_p[..., :kv_dim].reshape(batch, window, n_kv, hd))
            wv_p.append(kv_p[..., kv_dim:].reshape(batch, window, n_kv, hd))
            os_, wk, wv = attn_sample(
                qkv_s[:, :q_dim].reshape(nb, n_q, hd),
                qkv_s[:, q_dim:q_dim + kv_dim].reshape(nb, 1, kv_dim),
                qkv_s[:, q_dim + kv_dim:].reshape(nb, 1, kv_dim),
                cache_k[a].reshape(nb, window, kv_dim), cache_v[a].reshape(nb, window, kv_dim),
                attn_sinks[a], n_kv, q_per_kv, hd, window)
            wk_s.append(wk.reshape(nb, window, n_kv, hd))
            wv_s.append(wv.reshape(nb, window, n_kv, hd))
            hp, hs = matmul(op, os_.reshape(nb, q_dim).astype(BF16), attn_w_o, a, bias=attn_b_o[a], res=hp, res_s=hs)
        else:
            r = i // 2
            up, us = matmul(hp, hs, hgrn_w_in, r, norm_w=norm_mix[i])
            ogp, sp = gla_prompt(up, hgrn_lower_bounds, hgrn_norm[r], i, batch, seq, n_heads, kd, vd)
            ogs, ss = gla_sample(us, state_hgrn[r], hgrn_lower_bounds, hgrn_norm[r], i, n_heads, kd, vd)
            st_p.append(sp)
            st_s.append(ss)
            hp, hs = matmul(ogp, ogs.reshape(nb, d).astype(BF16), hgrn_w_o, r, res=hp, res_s=hs)
        final_w = norm_final if i == depth - 1 else None
        hp, hs = ffn(hp, hs, norm_ffn[i], ffn_w_in, ffn_w_out, i, final_w=final_w)

    y_prompt = hp.reshape(batch, seq, d)
    y_sample = hs.reshape(nb, 1, d)
    return (y_prompt, y_sample, jnp.stack(wk_p), jnp.stack(wv_p), jnp.stack(st_p), jnp.stack(wk_s), jnp.stack(wv_s),
            jnp.stack(st_s))
```

```python
import functools

import jax
import jax.numpy as jnp
from jax import lax
from jax.experimental import pallas as pl
from jax.experimental.pallas import tpu as pltpu

RMS_EPS = 1e-5
BF16 = jnp.bfloat16
F32 = jnp.float32

V7X_VMEM_LIMIT_BYTES = 56 * 1024 * 1024
V7X_WIDE_VMEM_LIMIT_BYTES = 60 * 1024 * 1024
GLA_CHUNK = 128
GLA_HEAD_GROUP = 16
SUBLANES = 8
BF16_SUBLANES = 16
LANES = 128
LOG2E = 1.4426950408889634


def _params(*sem):
    return pltpu.CompilerParams(dimension_semantics=sem, vmem_limit_bytes=V7X_VMEM_LIMIT_BYTES)


def _tile(n, pref):
    return pref if n % pref == 0 else n


def _rmsnorm_rows(x, w):
    ms = jnp.mean(x * x, axis=-1, keepdims=True)
    return (x * lax.rsqrt(ms + RMS_EPS)) * w


NORM_ROWS = 16
NORM_UNROLL = 4


def _norm_rows_into(src_ref, nw_ref, dst_ref, copy_ref=None):
    total = src_ref.shape[0]
    rows = min(NORM_ROWS, total)

    def body(r, carry):
        rs = pl.ds(pl.multiple_of(r * rows, rows), rows)
        h = src_ref[rs, :]
        dst_ref[rs, :] = _rmsnorm_rows(h, nw_ref[...]).astype(BF16)
        if copy_ref is not None:
            copy_ref[rs, :] = h
        return carry

    lax.fori_loop(0, total // rows, body, 0, unroll=min(NORM_UNROLL, total // rows))


TILE_LOAD_CHUNKS = 8


def _load_chunks(total):
    return TILE_LOAD_CHUNKS if total % (TILE_LOAD_CHUNKS * NORM_ROWS) == 0 else 1


def _matmul_kernel(*refs, has_norm, has_bias, has_res):
    x_ref, xs_ref, w_ref = refs[0], refs[1], refs[2]
    pos = 3
    nw_ref = b_ref = r_ref = rs_ref = None
    if has_norm:
        nw_ref = refs[pos]
        pos += 1
    if has_bias:
        b_ref = refs[pos]
        pos += 1
    if has_res:
        r_ref, rs_ref = refs[pos], refs[pos + 1]
        pos += 2
    o_ref, os_ref = refs[pos], refs[pos + 1]
    i, j = pl.program_id(0), pl.program_id(1)
    if has_norm:
        xn_ref, xsn_ref, stage_ref, x_sem = refs[pos + 2:pos + 6]
        tm = xn_ref.shape[0]

        @pl.when(j == 0)
        def _():
            n = _load_chunks(tm)
            rows = tm // n
            copies = [pltpu.make_async_copy(x_ref.at[pl.ds(pl.multiple_of(i * tm + c * rows, rows), rows), :],
                                            stage_ref.at[c * rows:(c + 1) * rows], x_sem.at[c]) for c in range(n)]
            for cp in copies:
                cp.start()
            for c, cp in enumerate(copies):
                cp.wait()
                _norm_rows_into(stage_ref.at[c * rows:(c + 1) * rows], nw_ref, xn_ref.at[c * rows:(c + 1) * rows])

        @pl.when((j == 0) & (i == 0))
        def _():
            _norm_rows_into(xs_ref, nw_ref, xsn_ref)
    else:
        xn_ref, xsn_ref = x_ref, xs_ref
    w = w_ref[...].astype(BF16)

    def contract(lhs_ref):
        return jnp.dot(lhs_ref[...], w, preferred_element_type=F32)

    def finish(acc, res_ref, out_ref):
        if has_bias:
            acc = acc + b_ref[...]
        if has_res:
            acc = acc + res_ref[...]
        out_ref[...] = acc.astype(out_ref.dtype)

    finish(contract(xn_ref), r_ref, o_ref)

    @pl.when(i == 0)
    def _():
        finish(contract(xsn_ref), rs_ref, os_ref)


def matmul(x, xs, w, layer, norm_w=None, bias=None, res=None, res_s=None, out_dtype=F32):
    m, k = x.shape
    ms = xs.shape[0]
    n = w.shape[2]
    has_norm = norm_w is not None
    tm = _tile(m, 1024)
    tn = _tile(n, 512)
    nj = n // tn
    js = lambda i, j: jnp.where(i == 0, j, nj - 1)
    in_specs = [
        pl.BlockSpec(memory_space=pl.ANY) if has_norm else pl.BlockSpec((tm, k), lambda i, j: (i, 0)),
        pl.BlockSpec((ms, k), lambda i, j: (0, 0)),
        pl.BlockSpec((None, k, tn), lambda i, j: (layer, 0, j)),
    ]
    args = [x, xs, w]
    scratch = []
    if has_norm:
        in_specs.append(pl.BlockSpec((1, k), lambda i, j: (0, 0)))
        args.append(norm_w.reshape(1, k))
        scratch = [pltpu.VMEM((tm, k), BF16), pltpu.VMEM((ms, k), BF16),
                   pltpu.VMEM((tm, k), F32), pltpu.SemaphoreType.DMA((_load_chunks(tm),))]
    if bias is not None:
        in_specs.append(pl.BlockSpec((1, tn), lambda i, j: (0, j)))
        args.append(bias.reshape(1, n))
    if res is not None:
        in_specs.append(pl.BlockSpec((tm, tn), lambda i, j: (i, j)))
        in_specs.append(pl.BlockSpec((ms, tn), lambda i, j: (0, js(i, j))))
        args += [res, res_s]
    return pl.pallas_call(
        functools.partial(_matmul_kernel, has_norm=has_norm, has_bias=bias is not None, has_res=res is not None),
        out_shape=(jax.ShapeDtypeStruct((m, n), out_dtype), jax.ShapeDtypeStruct((ms, n), out_dtype)),
        grid=(m // tm, nj),
        in_specs=in_specs,
        out_specs=(pl.BlockSpec((tm, tn), lambda i, j: (i, j)), pl.BlockSpec((ms, tn), lambda i, j: (0, js(i, j)))),
        scratch_shapes=scratch,
        compiler_params=_params("arbitrary", "arbitrary"),
        name="matmul",
    )(*args)


FFN_ROWS = 1024
FFN_OUT_COLS = 512


def _final_norm_in_place(o_ref, fw_ref):
    total = o_ref.shape[0]
    rows = min(NORM_ROWS, total)

    def body(r, carry):
        rs = pl.ds(pl.multiple_of(r * rows, rows), rows)
        o_ref[rs, :] = _rmsnorm_rows(o_ref[rs, :], fw_ref[...])
        return carry

    lax.fori_loop(0, total // rows, body, 0, unroll=min(NORM_UNROLL, total // rows))


def _ffn_kernel(*refs, final_norm):
    h_hbm, hs_ref, nw_ref = refs[0], refs[1], refs[2]
    fw_ref = refs[3] if final_norm else None
    wg_ref, wu_ref, wo_ref, o_ref, os_ref, xn_ref, h_sem = refs[-7:]
    i, f = pl.program_id(0), pl.program_id(1)
    tm, d = o_ref.shape
    ms = os_ref.shape[0]
    last = pl.num_programs(1) - 1

    @pl.when(f == 0)
    def _():
        n = _load_chunks(tm)
        rows = tm // n
        copies = [pltpu.make_async_copy(h_hbm.at[pl.ds(pl.multiple_of(i * tm + c * rows, rows), rows), :],
                                        o_ref.at[c * rows:(c + 1) * rows], h_sem.at[c]) for c in range(n)]
        for cp in copies:
            cp.start()
        for c, cp in enumerate(copies):
            cp.wait()
            _norm_rows_into(o_ref.at[c * rows:(c + 1) * rows], nw_ref, xn_ref.at[c * rows:(c + 1) * rows])

    @pl.when((f == 0) & (i == 0))
    def _():
        _norm_rows_into(hs_ref, nw_ref, xn_ref.at[tm:tm + ms], copy_ref=os_ref)

    def step(rows):
        xn = xn_ref[0:rows, :]
        g = jnp.dot(xn, wg_ref[...].astype(BF16), preferred_element_type=F32)
        u = jnp.dot(xn, wu_ref[...].astype(BF16), preferred_element_type=F32)
        act = ((g * jax.nn.sigmoid(g)) * u).astype(BF16)
        for c in range(0, d, FFN_OUT_COLS):
            cs = slice(c, min(c + FFN_OUT_COLS, d))
            r = jnp.dot(act, wo_ref[:, cs].astype(BF16), preferred_element_type=F32)
            o_ref[:, cs] += r[0:tm]
            if rows > tm:
                os_ref[:, cs] += r[tm:rows]

    @pl.when(i == 0)
    def _():
        step(tm + ms)

    @pl.when(i != 0)
    def _():
        step(tm)

    if final_norm:
        @pl.when(f == last)
        def _():
            _final_norm_in_place(o_ref, fw_ref)

        @pl.when((f == last) & (i == 0))
        def _():
            _final_norm_in_place(os_ref, fw_ref)


def ffn(h, hs, norm_w, w_in, w_out, layer, final_w=None):
    m, d = h.shape
    ms = hs.shape[0]
    d_ff = w_out.shape[1]
    tm = _tile(m, FFN_ROWS)
    tf = _tile(d_ff, 256)
    nf = d_ff // tf
    assert tm % BF16_SUBLANES == 0 and ms % BF16_SUBLANES == 0
    once = pl.Buffered(1)
    vec = pl.BlockSpec((1, d), lambda i, f: (0, 0), pipeline_mode=once)
    small = pl.BlockSpec((ms, d), lambda i, f: (0, 0), pipeline_mode=once)
    final_norm = final_w is not None
    return pl.pallas_call(
        functools.partial(_ffn_kernel, final_norm=final_norm),
        out_shape=(jax.ShapeDtypeStruct((m, d), F32), jax.ShapeDtypeStruct((ms, d), F32)),
        grid=(m // tm, nf),
        in_specs=[pl.BlockSpec(memory_space=pl.ANY), small, vec] + [vec] * final_norm + [
            pl.BlockSpec((None, d, tf), lambda i, f: (layer, 0, f)),
            pl.BlockSpec((None, d, tf), lambda i, f: (layer, 0, f + nf)),
            pl.BlockSpec((None, tf, d), lambda i, f: (layer, f, 0)),
        ],
        out_specs=(pl.BlockSpec((tm, d), lambda i, f: (i, 0), pipeline_mode=once),
                   pl.BlockSpec((ms, d), lambda i, f: (0, 0))),
        scratch_shapes=[pltpu.VMEM((tm + ms, d), BF16), pltpu.SemaphoreType.DMA((_load_chunks(tm),))],
        compiler_params=pltpu.CompilerParams(
            dimension_semantics=("arbitrary", "arbitrary"), vmem_limit_bytes=V7X_WIDE_VMEM_LIMIT_BYTES),
        name="ffn",
    )(h, hs, norm_w.reshape(1, d), *([final_w.reshape(1, d)] if final_norm else []), w_in, w_in, w_out)


def _attn_prompt_kernel(sink_ref, q_ref, kc_ref, kp_ref, vc_ref, vp_ref, o_ref, *, n_kv, q_per_kv, hd, window):
    nb = pl.program_id(1)
    w = window
    assert 2 * hd == LANES and q_per_kv % 2 == 0 and n_kv % 2 == 0
    npair = q_per_kv // 2
    cols = npair * w
    q_low = lax.broadcasted_iota(jnp.int32, (w, LANES), 1) < hd
    kv_low = lax.broadcasted_iota(jnp.int32, (2 * w, LANES), 1) < hd
    kv_row0 = lax.broadcasted_iota(jnp.int32, (2 * w, LANES), 0) == 0
    kj = lax.broadcasted_iota(jnp.int32, (w, cols), 0)
    qi = lax.broadcasted_iota(jnp.int32, (w, cols), 1) & (w - 1)
    from_prev = kj > qi
    top_row0 = lax.broadcasted_iota(jnp.int32, (SUBLANES, cols), 0) == 0
    no_prev = jnp.where(nb > 0, 0.0, -jnp.inf)
    q_scale = hd ** -0.5 * LOG2E
    for h in range(n_kv):
        slab, half = divmod(h, 2)
        ksl = slice(slab * LANES, (slab + 1) * LANES)
        kk = [jnp.concatenate([kp_ref[:, ksl], kc_ref[:, ksl]], axis=0)]
        vv = [jnp.concatenate([vp_ref[:, ksl], vc_ref[:, ksl]], axis=0)]
        kk.append(pltpu.roll(kk[0], hd, axis=1))
        vv.append(pltpu.roll(vv[0], hd, axis=1))
        res = []
        for par in range(2):
            keep = q_low if par == 0 else ~q_low
            q = jnp.concatenate(
                [jnp.where(keep, q_ref[:, (h * npair + p) * LANES:(h * npair + p + 1) * LANES] * q_scale, 0.0)
                 for p in range(npair)], axis=0).astype(BF16)
            swap = int(par != half)
            s = lax.dot_general(kk[swap].astype(BF16), q, (((1,), (1,)), ((), ())), preferred_element_type=F32)
            s = jnp.where(from_prev, s[:w] + no_prev, s[w:])
            sink = jnp.concatenate(
                [jnp.full((1, w), sink_ref[h * q_per_kv + 2 * p + par] * LOG2E, F32) for p in range(npair)], axis=1)
            m = jnp.maximum(jnp.max(s, axis=0, keepdims=True), sink)
            e = jnp.exp2(s - m)
            e_prev = jnp.where(from_prev, e, 0.0)
            e_top = jnp.where(top_row0, jnp.exp2(sink - m), e_prev[:SUBLANES])
            e2 = jnp.concatenate([e_top, e_prev[SUBLANES:], jnp.where(from_prev, 0.0, e)], axis=0).astype(BF16)
            own = kv_low if par == 0 else ~kv_low
            v_aug = jnp.where(own, jnp.where(kv_row0, 0.0, vv[swap]), 1.0).astype(BF16)
            res.append(lax.dot_general(e2, v_aug, (((0,), (0,)), ((), ())), preferred_element_type=F32))
        for p in range(npair):
            r0, r1 = res[0][p * w:(p + 1) * w], res[1][p * w:(p + 1) * w]
            num = jnp.where(q_low, r0, r1)
            den = pltpu.roll(jnp.where(q_low, r1, r0), hd, axis=1)
            c = (h * npair + p) * LANES
            o_ref[:, c:c + LANES] = (num * (1.0 / den)).astype(o_ref.dtype)


def attn_prompt(qkv, sinks, batch, seq, n_kv, q_per_kv, hd, window):
    q_dim = n_kv * q_per_kv * hd
    kv_dim = n_kv * hd
    nblk = seq // window
    kcol = q_dim // kv_dim
    cur = lambda b, n: b * nblk + n
    prev = lambda b, n: b * nblk + jnp.maximum(n - 1, 0)
    return pl.pallas_call(
        functools.partial(_attn_prompt_kernel, n_kv=n_kv, q_per_kv=q_per_kv, hd=hd, window=window),
        out_shape=jax.ShapeDtypeStruct((batch * seq, q_dim), BF16),
        grid=(batch, nblk),
        in_specs=[
            pl.BlockSpec(memory_space=pltpu.SMEM),
            pl.BlockSpec((window, q_dim), lambda b, n: (cur(b, n), 0)),
            pl.BlockSpec((window, kv_dim), lambda b, n: (cur(b, n), kcol)),
            pl.BlockSpec((window, kv_dim), lambda b, n: (prev(b, n), kcol)),
            pl.BlockSpec((window, kv_dim), lambda b, n: (cur(b, n), kcol + 1)),
            pl.BlockSpec((window, kv_dim), lambda b, n: (prev(b, n), kcol + 1)),
        ],
        out_specs=pl.BlockSpec((window, q_dim), lambda b, n: (cur(b, n), 0)),
        compiler_params=_params("parallel", "arbitrary"),
        name="attn_prompt",
    )(sinks, qkv, qkv, qkv, qkv, qkv)


ATTN_SAMPLE_BLOCK = 4


def _attn_sample_kernel(sink_ref, q_ref, kn_ref, vn_ref, ck_ref, cv_ref, o_ref, wk_ref, wv_ref, *, n_kv, q_per_kv, hd,
                        window):
    w = window
    scale = hd ** -0.5
    rows = [slice(h * q_per_kv, (h + 1) * q_per_kv) for h in range(n_kv)]
    cols = [slice(h * hd, (h + 1) * hd) for h in range(n_kv)]
    sinks = [sink_ref[rs, :] for rs in rows]
    for b in range(q_ref.shape[0]):
        wk_ref[b, 0:w - 1, :] = ck_ref[b, 1:w, :]
        wk_ref[b, w - 1:w, :] = kn_ref[b]
        wv_ref[b, 0:w - 1, :] = cv_ref[b, 1:w, :]
        wv_ref[b, w - 1:w, :] = vn_ref[b]
        ss = [lax.dot_general((q_ref[b, rs, :] * scale).astype(BF16), wk_ref[b, :, ks].astype(BF16),
                              (((1,), (1,)), ((), ())), preferred_element_type=F32) for rs, ks in zip(rows, cols)]
        ms = [jnp.maximum(jnp.max(s, axis=-1, keepdims=True), sink) for s, sink in zip(ss, sinks)]
        es = [jnp.exp(s - m) for s, m in zip(ss, ms)]
        dens = [jnp.sum(e, axis=-1, keepdims=True) + jnp.exp(sink - m) for e, sink, m in zip(es, sinks, ms)]
        os_ = [jnp.dot(e.astype(BF16), wv_ref[b, :, ks].astype(BF16), preferred_element_type=F32)
               for e, ks in zip(es, cols)]
        for rs, o, den in zip(rows, os_, dens):
            o_ref[b, rs, :] = o * (1.0 / den)


def attn_sample(q, k_new, v_new, cache_k, cache_v, sinks, n_kv, q_per_kv, hd, window):
    nb = q.shape[0]
    nq = n_kv * q_per_kv
    kv_dim = n_kv * hd
    sb = _tile(nb, ATTN_SAMPLE_BLOCK)
    win = pl.BlockSpec((sb, window, kv_dim), lambda i: (i, 0, 0))
    row = pl.BlockSpec((sb, 1, kv_dim), lambda i: (i, 0, 0))
    qs = pl.BlockSpec((sb, nq, hd), lambda i: (i, 0, 0))
    return pl.pallas_call(
        functools.partial(_attn_sample_kernel, n_kv=n_kv, q_per_kv=q_per_kv, hd=hd, window=window),
        out_shape=(
            jax.ShapeDtypeStruct((nb, nq, hd), F32),
            jax.ShapeDtypeStruct((nb, window, kv_dim), F32),
            jax.ShapeDtypeStruct((nb, window, kv_dim), F32),
        ),
        grid=(nb // sb,),
        in_specs=[pl.BlockSpec((nq, 1), lambda i: (0, 0)), qs, row, row, win, win],
        out_specs=(qs, win, win),
        compiler_params=_params("parallel"),
        name="attn_sample",
    )(sinks.reshape(nq, 1), q, k_new, v_new, cache_k, cache_v)


def _lower_bound(l, layer):
    e = jnp.exp(l - jnp.max(l, axis=0, keepdims=True))
    p = e / jnp.sum(e, axis=0, keepdims=True)
    return jnp.sum(p[0:layer + 1], axis=0, keepdims=True) - p[0:1]


def _cumsum_rows(x, tri):
    k = x.shape[1]
    hi = x.astype(BF16)
    r1 = x - hi.astype(F32)
    mid = r1.astype(BF16)
    lo = (r1 - mid.astype(F32)).astype(BF16)
    parts = jnp.dot(tri, jnp.concatenate([hi, mid, lo], axis=1), preferred_element_type=F32)
    return (parts[:, 0:k] + parts[:, k:2 * k]) + parts[:, 2 * k:3 * k]


def _block_ref_rows(b, half):
    c, k = b.shape
    blk = 2 * half
    if blk >= 8:
        b3 = b.reshape(c // blk, blk, k)
        return jnp.broadcast_to(b3[:, half - 1:half, :], (c // blk, blk, k)).reshape(c, k)
    b3 = b.reshape(c // 8, 8, k)
    sub = lax.broadcasted_iota(jnp.int32, (c // 8, 8, k), 1)
    out = None
    for j in range(8 // blk):
        r = j * blk + half - 1
        cand = jnp.broadcast_to(b3[:, r:r + 1, :], (c // 8, 8, k))
        out = cand if out is None else jnp.where(sub >= j * blk, cand, out)
    return out.reshape(c, k)


def _gla_prompt_kernel(lb_ref, nw_ref, q_ref, f_ref, i_ref, g_ref, og_ref, st_ref, s_scr, *, layer, n_heads, kd, vd):
    c = q_ref.shape[0]
    ci = pl.program_id(1)
    n_chunks = pl.num_programs(1)

    @pl.when(ci == 0)
    def _():
        s_scr[...] = jnp.zeros_like(s_scr)

    nw = nw_ref[...]
    ti = lax.broadcasted_iota(jnp.int32, (c, c), 0)
    si = lax.broadcasted_iota(jnp.int32, (c, c), 1)
    pair_bits = jnp.where(ti > si, ti ^ si, 0)
    tri = (ti >= si).astype(BF16)
    lane = lax.broadcasted_iota(jnp.int32, (SUBLANES, c), 1)
    q_scale = kd ** -0.5
    zero_rows = {hf: jnp.zeros((hf, kd), F32) for hf in (SUBLANES << j for j in range(8)) if hf < c}

    def prep(h):
        ks = pl.ds(pl.multiple_of(h * kd, kd), kd)
        vs = pl.ds(pl.multiple_of(h * vd, vd), vd)
        lb = _lower_bound(lb_ref[:, ks], layer)
        qr = q_ref[:, ks]
        q = (qr * jax.nn.sigmoid(qr)) * q_scale
        forget = lb + (1.0 - lb) * jax.nn.sigmoid(f_ref[:, ks])
        v = i_ref[:, vs]
        b2 = _cumsum_rows(jnp.log2(forget), tri)
        return dict(h=h, vs=vs, q=q, k=1.0 - forget, forget=forget, v=v, v16=v.astype(BF16), b2=b2)

    def carry_state(t):
        q, k, b2, v16, h = t["q"], t["k"], t["b2"], t["v16"], t["h"]
        b_last = b2[c - 1:c, :]
        s_t = s_scr[h]
        t["o"] = lax.dot_general((q * jnp.exp2(b2)).astype(BF16), s_t.astype(BF16), (((1,), (1,)), ((), ())),
                                 preferred_element_type=F32)
        kdec = (k * jnp.exp2(b_last - b2)).astype(BF16)
        s_scr[h] = jnp.exp2(b_last) * s_t + lax.dot_general(v16, kdec, (((0,), (0,)), ((), ())),
                                                            preferred_element_type=F32)
        t["a"] = jnp.zeros((c, c), F32)

    def small_level(t, hf):
        q, k, b2 = t["q"], t["k"], t["b2"]
        if hf == 1:
            qe, ke = q * t["forget"], k
        else:
            e = jnp.exp2(-jnp.abs(b2 - _block_ref_rows(b2, hf)))
            qe, ke = q * e, k * e
        al = lax.dot_general(qe.astype(BF16), ke.astype(BF16), (((1,), (1,)), ((), ())),
                             preferred_element_type=F32)
        t["a"] = jnp.where(pair_bits >= hf, al, t["a"])

    def big_level(t, hf):
        q, k, b2, a_rows = t["q"], t["k"], t["b2"], t["a_rows"]
        q_parts, k_parts, uppers = [], [], []
        for blk in range(0, c, 2 * hf):
            lo, up = slice(blk, blk + hf), slice(blk + hf, blk + 2 * hf)
            r = b2[blk + hf - 1:blk + hf, :]
            q_parts.append(q[up] * jnp.exp2(b2[up] - r))
            k_parts += [k[lo] * jnp.exp2(r - b2[lo]), zero_rows[hf]]
            uppers += [(p, blk) for p in range((blk + hf) // SUBLANES, (blk + 2 * hf) // SUBLANES)]
        al = lax.dot_general(jnp.concatenate(q_parts, axis=0).astype(BF16),
                             jnp.concatenate(k_parts, axis=0).astype(BF16),
                             (((1,), (1,)), ((), ())), preferred_element_type=F32)
        for j, (p, blk) in enumerate(uppers):
            same_block = (lane >= blk) & (lane < blk + hf)
            a_rows[p] = jnp.where(same_block, al[j * SUBLANES:(j + 1) * SUBLANES], a_rows[p])

    def finish(t):
        q, k, v = t["q"], t["k"], t["v"]
        o = t["o"] + jnp.dot(jnp.concatenate(t["a_rows"], axis=0).astype(BF16), t["v16"], preferred_element_type=F32)
        o = o + jnp.sum(q * k, axis=-1, keepdims=True) * v
        o = (o * lax.rsqrt(jnp.mean(o * o, axis=-1, keepdims=True) + RMS_EPS)) * nw
        gr = g_ref[:, t["vs"]]
        og_ref[:, t["vs"]] = (o * (gr * jax.nn.sigmoid(gr))).astype(og_ref.dtype)

    group = min(GLA_HEAD_GROUP, n_heads)

    def head_group(hg, carry):
        ts = [prep(hg * group + j) for j in range(group)]
        for t in ts:
            carry_state(t)
        hf = 1
        while hf < min(SUBLANES, c):
            for t in ts:
                small_level(t, hf)
            hf *= 2
        for t in ts:
            t["a_rows"] = [t["a"][r:r + SUBLANES] for r in range(0, c, SUBLANES)]
        while hf < c:
            for t in ts:
                big_level(t, hf)
            hf *= 2
        for t in ts:
            finish(t)
        return carry

    lax.fori_loop(0, n_heads // group, head_group, 0)

    @pl.when(ci == n_chunks - 1)
    def _():
        def emit(h, carry):
            st_ref[0, h] = s_scr[h].T
            return carry

        lax.fori_loop(0, n_heads, emit, 0)


def gla_prompt(u, lower_bounds, norm_w, layer, batch, seq, n_heads, kd, vd):
    d = n_heads * kd
    c = _tile(seq, GLA_CHUNK)
    nch = seq // c
    n_layers = lower_bounds.shape[0]
    col = lambda j: pl.BlockSpec((c, d), lambda b, i: (b * nch + i, j))
    return pl.pallas_call(
        functools.partial(_gla_prompt_kernel, layer=layer, n_heads=n_heads, kd=kd, vd=vd),
        out_shape=(
            jax.ShapeDtypeStruct((batch * seq, d), BF16),
            jax.ShapeDtypeStruct((batch, n_heads, kd, vd), F32),
        ),
        grid=(batch, nch),
        in_specs=[
            pl.BlockSpec((n_layers, d), lambda b, i: (0, 0)),
            pl.BlockSpec((1, vd), lambda b, i: (0, 0)),
            col(0), col(1), col(2), col(3),
        ],
        out_specs=(
            pl.BlockSpec((c, d), lambda b, i: (b * nch + i, 0)),
            pl.BlockSpec((1, n_heads, kd, vd), lambda b, i: (b, 0, 0, 0)),
        ),
        scratch_shapes=[pltpu.VMEM((n_heads, vd, kd), F32)],
        compiler_params=_params("parallel", "arbitrary"),
        name="gla_prompt",
    )(lower_bounds, norm_w.reshape(1, vd), u, u, u, u)


def _gla_sample_kernel(lb_ref, nw_ref, x_ref, s_ref, og_ref, so_ref, *, layer, n_heads, kd):
    nh = n_heads
    x = x_ref[0]
    lb = _lower_bound(lb_ref[...], layer)[0]
    qr = x[0:nh]
    q = (qr * jax.nn.sigmoid(qr)) * (kd ** -0.5)
    forget = lb + (1.0 - lb) * jax.nn.sigmoid(x[nh:2 * nh])
    k = 1.0 - forget
    v = x[2 * nh:3 * nh]
    gr = x[3 * nh:4 * nh]
    pad = jnp.zeros((x.shape[0] - 3 * nh, kd), F32)
    cols = jnp.concatenate([q, forget, k, pad], axis=0).T
    outs = []
    for h in range(nh):
        s_new = cols[:, nh + h:nh + h + 1] * s_ref[0, h] + cols[:, 2 * nh + h:2 * nh + h + 1] * v[h:h + 1, :]
        so_ref[0, h] = s_new
        outs.append(jnp.sum(cols[:, h:h + 1] * s_new, axis=0, keepdims=True))
    o = jnp.concatenate(outs, axis=0)
    o = (o * lax.rsqrt(jnp.mean(o * o, axis=-1, keepdims=True) + RMS_EPS)) * nw_ref[...]
    og_ref[0] = o * (gr * jax.nn.sigmoid(gr))


def gla_sample(u, state, lower_bounds, norm_w, layer, n_heads, kd, vd):
    nb = u.shape[0]
    n_layers = lower_bounds.shape[0]
    rows = 4 * n_heads
    return pl.pallas_call(
        functools.partial(_gla_sample_kernel, layer=layer, n_heads=n_heads, kd=kd),
        out_shape=(
            jax.ShapeDtypeStruct((nb, n_heads, vd), F32),
            jax.ShapeDtypeStruct((nb, n_heads, kd, vd), F32),
        ),
        grid=(nb,),
        in_specs=[
            pl.BlockSpec((n_layers, n_heads, kd), lambda i: (0, 0, 0)),
            pl.BlockSpec((1, vd), lambda i: (0, 0)),
            pl.BlockSpec((1, rows, kd), lambda i: (i, 0, 0)),
            pl.BlockSpec((1, n_heads, kd, vd), lambda i: (i, 0, 0, 0)),
        ],
        out_specs=(
            pl.BlockSpec((1, n_heads, vd), lambda i: (i, 0, 0)),
            pl.BlockSpec((1, n_heads, kd, vd), lambda i: (i, 0, 0, 0)),
        ),
        compiler_params=_params("parallel"),
        name="gla_sample",
    )(lower_bounds.reshape(n_layers, n_heads, kd), norm_w.reshape(1, vd), u.reshape(nb, rows, kd), state)


def kernel(x_prompt, x_sample, cache_k, cache_v, state_hgrn, norm_mix, norm_ffn, norm_final, attn_w_qkv, attn_b_qkv,
           attn_sinks, attn_w_o, attn_b_o, hgrn_w_in, hgrn_lower_bounds, hgrn_norm, hgrn_w_o, ffn_w_in, ffn_w_out):
    batch, seq, d = x_prompt.shape
    nb = x_sample.shape[0]
    assert x_sample.shape[1] == 1
    depth = norm_mix.shape[0]
    window, n_kv, hd = cache_k.shape[2:]
    n_q = attn_sinks.shape[1]
    q_per_kv = n_q // n_kv
    q_dim, kv_dim = n_q * hd, n_kv * hd
    n_heads, kd, vd = state_hgrn.shape[2:]

    hp = x_prompt.reshape(batch * seq, d)
    hs = x_sample.reshape(nb, d)
    wk_p, wv_p, st_p, wk_s, wv_s, st_s = [], [], [], [], [], []
    for i in range(depth):
        if i % 2 == 0:
            a = i // 2
            qkv_p, qkv_s = matmul(hp, hs, attn_w_qkv, a, norm_w=norm_mix[i], bias=attn_b_qkv[a])
            op = attn_prompt(qkv_p, attn_sinks[a], batch, seq, n_kv, q_per_kv, hd, window)
            kv_p = qkv_p.reshape(batch, seq, q_dim + 2 * kv_dim)[:, seq - window:, q_dim:]
            wk_p.append(kv_p[..., :kv_dim].reshape(batch, window, n_kv, hd))
            wv_p.append(kv_p[..., kv_dim:].reshape(batch, window, n_kv, hd))
            os_, wk, wv = attn_sample(
                qkv_s[:, :q_dim].reshape(nb, n_q, hd),
                qkv_s[:, q_dim:q_dim + kv_dim].reshape(nb, 1, kv_dim),
                qkv_s[:, q_dim + kv_dim:].reshape(nb, 1, kv_dim),
                cache_k[a].reshape(nb, window, kv_dim), cache_v[a].reshape(nb, window, kv_dim),
                attn_sinks[a], n_kv, q_per_kv, hd, window)
            wk_s.append(wk.reshape(nb, window, n_kv, hd))
            wv_s.append(wv.reshape(nb, window, n_kv, hd))
            hp, hs = matmul(op, os_.reshape(nb, q_dim).astype(BF16), attn_w_o, a, bias=attn_b_o[a], res=hp, res_s=hs)
        else:
            r = i // 2
            up, us = matmul(hp, hs, hgrn_w_in, r, norm_w=norm_mix[i])
            ogp, sp = gla_prompt(up, hgrn_lower_bounds, hgrn_norm[r], i, batch, seq, n_heads, kd, vd)
            ogs, ss = gla_sample(us, state_hgrn[r], hgrn_lower_bounds, hgrn_norm[r], i, n_heads, kd, vd)
            st_p.append(sp)
            st_s.append(ss)
            hp, hs = matmul(ogp, ogs.reshape(nb, d).astype(BF16), hgrn_w_o, r, res=hp, res_s=hs)
        final_w = norm_final if i == depth - 1 else None
        hp, hs = ffn(hp, hs, norm_ffn[i], ffn_w_in, ffn_w_out, i, final_w=final_w)

    y_prompt = hp.reshape(batch, seq, d)
    y_sample = hs.reshape(nb, 1, d)
    return (y_prompt, y_sample, jnp.stack(wk_p), jnp.stack(wv_p), jnp.stack(st_p), jnp.stack(wk_s), jnp.stack(wv_s),
            jnp.stack(st_s))
```

```python
import functools

import jax
import jax.numpy as jnp
from jax import lax
from jax.experimental import pallas as pl
from jax.experimental.pallas import tpu as pltpu

RMS_EPS = 1e-5
BF16 = jnp.bfloat16
F32 = jnp.float32

V7X_VMEM_LIMIT_BYTES = 56 * 1024 * 1024
V7X_WIDE_VMEM_LIMIT_BYTES = 60 * 1024 * 1024
GLA_CHUNK = 128
GLA_HEAD_GROUP = 16
SUBLANES = 8
BF16_SUBLANES = 16
LANES = 128
LOG2E = 1.4426950408889634


def _params(*sem):
    return pltpu.CompilerParams(dimension_semantics=sem, vmem_limit_bytes=V7X_VMEM_LIMIT_BYTES)


def _tile(n, pref):
    return pref if n % pref == 0 else n


def _rmsnorm_rows(x, w):
    ms = jnp.mean(x * x, axis=-1, keepdims=True)
    return (x * lax.rsqrt(ms + RMS_EPS)) * w


NORM_ROWS = 16
NORM_UNROLL = 4


def _norm_rows_into(src_ref, nw_ref, dst_ref, copy_ref=None):
    total = src_ref.shape[0]
    rows = min(NORM_ROWS, total)

    def body(r, carry):
        rs = pl.ds(pl.multiple_of(r * rows, rows), rows)
        h = src_ref[rs, :]
        dst_ref[rs, :] = _rmsnorm_rows(h, nw_ref[...]).astype(BF16)
        if copy_ref is not None:
            copy_ref[rs, :] = h
        return carry

    lax.fori_loop(0, total // rows, body, 0, unroll=min(NORM_UNROLL, total // rows))


TILE_LOAD_CHUNKS = 8


def _load_chunks(total):
    return TILE_LOAD_CHUNKS if total % (TILE_LOAD_CHUNKS * NORM_ROWS) == 0 else 1


def _matmul_kernel(*refs, has_norm, has_bias, has_res):
    x_ref, xs_ref, w_ref = refs[0], refs[1], refs[2]
    pos = 3
    nw_ref = b_ref = r_ref = rs_ref = None
    if has_norm:
        nw_ref = refs[pos]
        pos += 1
    if has_bias:
        b_ref = refs[pos]
        pos += 1
    if has_res:
        r_ref, rs_ref = refs[pos], refs[pos + 1]
        pos += 2
    o_ref, os_ref = refs[pos], refs[pos + 1]
    i, j = pl.program_id(0), pl.program_id(1)
    if has_norm:
        xn_ref, xsn_ref, stage_ref, x_sem = refs[pos + 2:pos + 6]
        tm = xn_ref.shape[0]

        @pl.when(j == 0)
        def _():
            n = _load_chunks(tm)
            rows = tm // n
            copies = [pltpu.make_async_copy(x_ref.at[pl.ds(pl.multiple_of(i * tm + c * rows, rows), rows), :],
                                            stage_ref.at[c * rows:(c + 1) * rows], x_sem.at[c]) for c in range(n)]
            for cp in copies:
                cp.start()
            for c, cp in enumerate(copies):
                cp.wait()
                _norm_rows_into(stage_ref.at[c * rows:(c + 1) * rows], nw_ref, xn_ref.at[c * rows:(c + 1) * rows])

        @pl.when((j == 0) & (i == 0))
        def _():
            _norm_rows_into(xs_ref, nw_ref, xsn_ref)
    else:
        xn_ref, xsn_ref = x_ref, xs_ref
    w = w_ref[...].astype(BF16)

    def contract(lhs_ref):
        return jnp.dot(lhs_ref[...], w, preferred_element_type=F32)

    def finish(acc, res_ref, out_ref):
        if has_bias:
            acc = acc + b_ref[...]
        if has_res:
            acc = acc + res_ref[...]
        out_ref[...] = acc.astype(out_ref.dtype)

    finish(contract(xn_ref), r_ref, o_ref)

    @pl.when(i == 0)
    def _():
        finish(contract(xsn_ref), rs_ref, os_ref)


def matmul(x, xs, w, layer, norm_w=None, bias=None, res=None, res_s=None, out_dtype=F32):
    m, k = x.shape
    ms = xs.shape[0]
    n = w.shape[2]
    has_norm = norm_w is not None
    tm = _tile(m, 1024)
    tn = _tile(n, 512)
    nj = n // tn
    js = lambda i, j: jnp.where(i == 0, j, nj - 1)
    in_specs = [
        pl.BlockSpec(memory_space=pl.ANY) if has_norm else pl.BlockSpec((tm, k), lambda i, j: (i, 0)),
        pl.BlockSpec((ms, k), lambda i, j: (0, 0)),
        pl.BlockSpec((None, k, tn), lambda i, j: (layer, 0, j)),
    ]
    args = [x, xs, w]
    scratch = []
    if has_norm:
        in_specs.append(pl.BlockSpec((1, k), lambda i, j: (0, 0)))
        args.append(norm_w.reshape(1, k))
        scratch = [pltpu.VMEM((tm, k), BF16), pltpu.VMEM((ms, k), BF16),
                   pltpu.VMEM((tm, k), F32), pltpu.SemaphoreType.DMA((_load_chunks(tm),))]
    if bias is not None:
        in_specs.append(pl.BlockSpec((1, tn), lambda i, j: (0, j)))
        args.append(bias.reshape(1, n))
    if res is not None:
        in_specs.append(pl.BlockSpec((tm, tn), lambda i, j: (i, j)))
        in_specs.append(pl.BlockSpec((ms, tn), lambda i, j: (0, js(i, j))))
        args += [res, res_s]
    return pl.pallas_call(
        functools.partial(_matmul_kernel, has_norm=has_norm, has_bias=bias is not None, has_res=res is not None),
        out_shape=(jax.ShapeDtypeStruct((m, n), out_dtype), jax.ShapeDtypeStruct((ms, n), out_dtype)),
        grid=(m // tm, nj),
        in_specs=in_specs,
        out_specs=(pl.BlockSpec((tm, tn), lambda i, j: (i, j)), pl.BlockSpec((ms, tn), lambda i, j: (0, js(i, j)))),
        scratch_shapes=scratch,
        compiler_params=_params("arbitrary", "arbitrary"),
        name="matmul",
    )(*args)


FFN_ROWS = 1024
FFN_OUT_COLS = 512
FINAL_NORM_ROWS = 32


def _final_norm_in_place(o_ref, fw_ref):
    total = o_ref.shape[0]
    rows = min(FINAL_NORM_ROWS, total)
    tiles = [(j, min(SUBLANES, rows - j)) for j in range(0, rows, SUBLANES)]

    def body(r, carry):
        base = pl.multiple_of(r * rows, rows)
        scales = []
        for j, n in tiles:
            x = o_ref[pl.ds(base + j, n), :]
            scales.append(lax.rsqrt(jnp.mean(x * x, axis=-1, keepdims=True) + RMS_EPS))
        for (j, n), scale in zip(tiles, scales):
            rs = pl.ds(base + j, n)
            o_ref[rs, :] = (o_ref[rs, :] * scale) * fw_ref[...]
        return carry

    lax.fori_loop(0, total // rows, body, 0)


def _ffn_kernel(*refs, final_norm):
    h_hbm, hs_ref, nw_ref = refs[0], refs[1], refs[2]
    fw_ref = refs[3] if final_norm else None
    wg_ref, wu_ref, wo_ref, o_ref, os_ref, xn_ref, h_sem = refs[-7:]
    i, f = pl.program_id(0), pl.program_id(1)
    tm, d = o_ref.shape
    ms = os_ref.shape[0]
    last = pl.num_programs(1) - 1

    @pl.when(f == 0)
    def _():
        n = _load_chunks(tm)
        rows = tm // n
        copies = [pltpu.make_async_copy(h_hbm.at[pl.ds(pl.multiple_of(i * tm + c * rows, rows), rows), :],
                                        o_ref.at[c * rows:(c + 1) * rows], h_sem.at[c]) for c in range(n)]
        for cp in copies:
            cp.start()
        for c, cp in enumerate(copies):
            cp.wait()
            _norm_rows_into(o_ref.at[c * rows:(c + 1) * rows], nw_ref, xn_ref.at[c * rows:(c + 1) * rows])

    @pl.when((f == 0) & (i == 0))
    def _():
        _norm_rows_into(hs_ref, nw_ref, xn_ref.at[tm:tm + ms], copy_ref=os_ref)

    def step(rows):
        xn = xn_ref[0:rows, :]
        g = jnp.dot(xn, wg_ref[...].astype(BF16), preferred_element_type=F32)
        u = jnp.dot(xn, wu_ref[...].astype(BF16), preferred_element_type=F32)
        act = ((g * jax.nn.sigmoid(g)) * u).astype(BF16)
        for c in range(0, d, FFN_OUT_COLS):
            cs = slice(c, min(c + FFN_OUT_COLS, d))
            r = jnp.dot(act, wo_ref[:, cs].astype(BF16), preferred_element_type=F32)
            o_ref[:, cs] += r[0:tm]
            if rows > tm:
                os_ref[:, cs] += r[tm:rows]

    @pl.when(i == 0)
    def _():
        step(tm + ms)

    @pl.when(i != 0)
    def _():
        step(tm)

    if final_norm:
        @pl.when(f == last)
        def _():
            _final_norm_in_place(o_ref, fw_ref)

        @pl.when((f == last) & (i == 0))
        def _():
            _final_norm_in_place(os_ref, fw_ref)


def ffn(h, hs, norm_w, w_in, w_out, layer, final_w=None):
    m, d = h.shape
    ms = hs.shape[0]
    d_ff = w_out.shape[1]
    tm = _tile(m, FFN_ROWS)
    tf = _tile(d_ff, 256)
    nf = d_ff // tf
    assert tm % BF16_SUBLANES == 0 and ms % BF16_SUBLANES == 0
    once = pl.Buffered(1)
    vec = pl.BlockSpec((1, d), lambda i, f: (0, 0), pipeline_mode=once)
    small = pl.BlockSpec((ms, d), lambda i, f: (0, 0), pipeline_mode=once)
    final_norm = final_w is not None
    return pl.pallas_call(
        functools.partial(_ffn_kernel, final_norm=final_norm),
        out_shape=(jax.ShapeDtypeStruct((m, d), F32), jax.ShapeDtypeStruct((ms, d), F32)),
        grid=(m // tm, nf),
        in_specs=[pl.BlockSpec(memory_space=pl.ANY), small, vec] + [vec] * final_norm + [
            pl.BlockSpec((None, d, tf), lambda i, f: (layer, 0, f)),
            pl.BlockSpec((None, d, tf), lambda i, f: (layer, 0, f + nf)),
            pl.BlockSpec((None, tf, d), lambda i, f: (layer, f, 0)),
        ],
        out_specs=(pl.BlockSpec((tm, d), lambda i, f: (i, 0), pipeline_mode=once),
                   pl.BlockSpec((ms, d), lambda i, f: (0, 0))),
        scratch_shapes=[pltpu.VMEM((tm + ms, d), BF16), pltpu.SemaphoreType.DMA((_load_chunks(tm),))],
        compiler_params=pltpu.CompilerParams(
            dimension_semantics=("arbitrary", "arbitrary"), vmem_limit_bytes=V7X_WIDE_VMEM_LIMIT_BYTES),
        name="ffn",
    )(h, hs, norm_w.reshape(1, d), *([final_w.reshape(1, d)] if final_norm else []), w_in, w_in, w_out)


def _attn_prompt_kernel(sink_ref, q_ref, kc_ref, kp_ref, vc_ref, vp_ref, o_ref, *, n_kv, q_per_kv, hd, window):
    nb = pl.program_id(1)
    w = window
    assert 2 * hd == LANES and q_per_kv % 2 == 0 and n_kv % 2 == 0
    npair = q_per_kv // 2
    cols = npair * w
    q_low = lax.broadcasted_iota(jnp.int32, (w, LANES), 1) < hd
    kv_low = lax.broadcasted_iota(jnp.int32, (2 * w, LANES), 1) < hd
    kv_row0 = lax.broadcasted_iota(jnp.int32, (2 * w, LANES), 0) == 0
    kj = lax.broadcasted_iota(jnp.int32, (w, cols), 0)
    qi = lax.broadcasted_iota(jnp.int32, (w, cols), 1) & (w - 1)
    from_prev = kj > qi
    top_row0 = lax.broadcasted_iota(jnp.int32, (SUBLANES, cols), 0) == 0
    no_prev = jnp.where(nb > 0, 0.0, -jnp.inf)
    q_scale = hd ** -0.5 * LOG2E
    for h in range(n_kv):
        slab, half = divmod(h, 2)
        ksl = slice(slab * LANES, (slab + 1) * LANES)
        kk = [jnp.concatenate([kp_ref[:, ksl], kc_ref[:, ksl]], axis=0)]
        vv = [jnp.concatenate([vp_ref[:, ksl], vc_ref[:, ksl]], axis=0)]
        kk.append(pltpu.roll(kk[0], hd, axis=1))
        vv.append(pltpu.roll(vv[0], hd, axis=1))
        res = []
        for par in range(2):
            keep = q_low if par == 0 else ~q_low
            q = jnp.concatenate(
                [jnp.where(keep, q_ref[:, (h * npair + p) * LANES:(h * npair + p + 1) * LANES] * q_scale, 0.0)
                 for p in range(npair)], axis=0).astype(BF16)
            swap = int(par != half)
            s = lax.dot_general(kk[swap].astype(BF16), q, (((1,), (1,)), ((), ())), preferred_element_type=F32)
            s = jnp.where(from_prev, s[:w] + no_prev, s[w:])
            sink = jnp.concatenate(
                [jnp.full((1, w), sink_ref[h * q_per_kv + 2 * p + par] * LOG2E, F32) for p in range(npair)], axis=1)
            m = jnp.maximum(jnp.max(s, axis=0, keepdims=True), sink)
            e = jnp.exp2(s - m)
            e_prev = jnp.where(from_prev, e, 0.0)
            e_top = jnp.where(top_row0, jnp.exp2(sink - m), e_prev[:SUBLANES])
            e2 = jnp.concatenate([e_top, e_prev[SUBLANES:], jnp.where(from_prev, 0.0, e)], axis=0).astype(BF16)
            own = kv_low if par == 0 else ~kv_low
            v_aug = jnp.where(own, jnp.where(kv_row0, 0.0, vv[swap]), 1.0).astype(BF16)
            res.append(lax.dot_general(e2, v_aug, (((0,), (0,)), ((), ())), preferred_element_type=F32))
        for p in range(npair):
            r0, r1 = res[0][p * w:(p + 1) * w], res[1][p * w:(p + 1) * w]
            num = jnp.where(q_low, r0, r1)
            den = pltpu.roll(jnp.where(q_low, r1, r0), hd, axis=1)
            c = (h * npair + p) * LANES
            o_ref[:, c:c + LANES] = (num * (1.0 / den)).astype(o_ref.dtype)


def attn_prompt(qkv, sinks, batch, seq, n_kv, q_per_kv, hd, window):
    q_dim = n_kv * q_per_kv * hd
    kv_dim = n_kv * hd
    nblk = seq // window
    kcol = q_dim // kv_dim
    cur = lambda b, n: b * nblk + n
    prev = lambda b, n: b * nblk + jnp.maximum(n - 1, 0)
    return pl.pallas_call(
        functools.partial(_attn_prompt_kernel, n_kv=n_kv, q_per_kv=q_per_kv, hd=hd, window=window),
        out_shape=jax.ShapeDtypeStruct((batch * seq, q_dim), BF16),
        grid=(batch, nblk),
        in_specs=[
            pl.BlockSpec(memory_space=pltpu.SMEM),
            pl.BlockSpec((window, q_dim), lambda b, n: (cur(b, n), 0)),
            pl.BlockSpec((window, kv_dim), lambda b, n: (cur(b, n), kcol)),
            pl.BlockSpec((window, kv_dim), lambda b, n: (prev(b, n), kcol)),
            pl.BlockSpec((window, kv_dim), lambda b, n: (cur(b, n), kcol + 1)),
            pl.BlockSpec((window, kv_dim), lambda b, n: (prev(b, n), kcol + 1)),
        ],
        out_specs=pl.BlockSpec((window, q_dim), lambda b, n: (cur(b, n), 0)),
        compiler_params=_params("parallel", "arbitrary"),
        name="attn_prompt",
    )(sinks, qkv, qkv, qkv, qkv, qkv)


ATTN_SAMPLE_BLOCK = 4


def _attn_sample_kernel(sink_ref, q_ref, kn_ref, vn_ref, ck_ref, cv_ref, o_ref, wk_ref, wv_ref, *, n_kv, q_per_kv, hd,
                        window):
    w = window
    scale = hd ** -0.5
    rows = [slice(h * q_per_kv, (h + 1) * q_per_kv) for h in range(n_kv)]
    cols = [slice(h * hd, (h + 1) * hd) for h in range(n_kv)]
    sinks = [sink_ref[rs, :] for rs in rows]
    for b in range(q_ref.shape[0]):
        wk_ref[b, 0:w - 1, :] = ck_ref[b, 1:w, :]
        wk_ref[b, w - 1:w, :] = kn_ref[b]
        wv_ref[b, 0:w - 1, :] = cv_ref[b, 1:w, :]
        wv_ref[b, w - 1:w, :] = vn_ref[b]
        ss = [lax.dot_general((q_ref[b, rs, :] * scale).astype(BF16), wk_ref[b, :, ks].astype(BF16),
                              (((1,), (1,)), ((), ())), preferred_element_type=F32) for rs, ks in zip(rows, cols)]
        ms = [jnp.maximum(jnp.max(s, axis=-1, keepdims=True), sink) for s, sink in zip(ss, sinks)]
        es = [jnp.exp(s - m) for s, m in zip(ss, ms)]
        dens = [jnp.sum(e, axis=-1, keepdims=True) + jnp.exp(sink - m) for e, sink, m in zip(es, sinks, ms)]
        os_ = [jnp.dot(e.astype(BF16), wv_ref[b, :, ks].astype(BF16), preferred_element_type=F32)
               for e, ks in zip(es, cols)]
        for rs, o, den in zip(rows, os_, dens):
            o_ref[b, rs, :] = o * (1.0 / den)


def attn_sample(q, k_new, v_new, cache_k, cache_v, sinks, n_kv, q_per_kv, hd, window):
    nb = q.shape[0]
    nq = n_kv * q_per_kv
    kv_dim = n_kv * hd
    sb = _tile(nb, ATTN_SAMPLE_BLOCK)
    win = pl.BlockSpec((sb, window, kv_dim), lambda i: (i, 0, 0))
    row = pl.BlockSpec((sb, 1, kv_dim), lambda i: (i, 0, 0))
    qs = pl.BlockSpec((sb, nq, hd), lambda i: (i, 0, 0))
    return pl.pallas_call(
        functools.partial(_attn_sample_kernel, n_kv=n_kv, q_per_kv=q_per_kv, hd=hd, window=window),
        out_shape=(
            jax.ShapeDtypeStruct((nb, nq, hd), F32),
            jax.ShapeDtypeStruct((nb, window, kv_dim), F32),
            jax.ShapeDtypeStruct((nb, window, kv_dim), F32),
        ),
        grid=(nb // sb,),
        in_specs=[pl.BlockSpec((nq, 1), lambda i: (0, 0)), qs, row, row, win, win],
        out_specs=(qs, win, win),
        compiler_params=_params("parallel"),
        name="attn_sample",
    )(sinks.reshape(nq, 1), q, k_new, v_new, cache_k, cache_v)


def _lower_bound(l, layer):
    e = jnp.exp(l - jnp.max(l, axis=0, keepdims=True))
    p = e / jnp.sum(e, axis=0, keepdims=True)
    return jnp.sum(p[0:layer + 1], axis=0, keepdims=True) - p[0:1]


def _cumsum_rows(x, tri):
    k = x.shape[1]
    hi = x.astype(BF16)
    r1 = x - hi.astype(F32)
    mid = r1.astype(BF16)
    lo = (r1 - mid.astype(F32)).astype(BF16)
    parts = jnp.dot(tri, jnp.concatenate([hi, mid, lo], axis=1), preferred_element_type=F32)
    return (parts[:, 0:k] + parts[:, k:2 * k]) + parts[:, 2 * k:3 * k]


def _block_ref_rows(b, half):
    c, k = b.shape
    blk = 2 * half
    if blk >= 8:
        b3 = b.reshape(c // blk, blk, k)
        return jnp.broadcast_to(b3[:, half - 1:half, :], (c // blk, blk, k)).reshape(c, k)
    b3 = b.reshape(c // 8, 8, k)
    sub = lax.broadcasted_iota(jnp.int32, (c // 8, 8, k), 1)
    out = None
    for j in range(8 // blk):
        r = j * blk + half - 1
        cand = jnp.broadcast_to(b3[:, r:r + 1, :], (c // 8, 8, k))
        out = cand if out is None else jnp.where(sub >= j * blk, cand, out)
    return out.reshape(c, k)


def _gla_prompt_kernel(lb_ref, nw_ref, q_ref, f_ref, i_ref, g_ref, og_ref, st_ref, s_scr, *, layer, n_heads, kd, vd):
    c = q_ref.shape[0]
    ci = pl.program_id(1)
    n_chunks = pl.num_programs(1)

    @pl.when(ci == 0)
    def _():
        s_scr[...] = jnp.zeros_like(s_scr)

    nw = nw_ref[...]
    ti = lax.broadcasted_iota(jnp.int32, (c, c), 0)
    si = lax.broadcasted_iota(jnp.int32, (c, c), 1)
    pair_bits = jnp.where(ti > si, ti ^ si, 0)
    tri = (ti >= si).astype(BF16)
    lane = lax.broadcasted_iota(jnp.int32, (SUBLANES, c), 1)
    q_scale = kd ** -0.5
    zero_rows = {hf: jnp.zeros((hf, kd), F32) for hf in (SUBLANES << j for j in range(8)) if hf < c}

    def prep(h):
        ks = pl.ds(pl.multiple_of(h * kd, kd), kd)
        vs = pl.ds(pl.multiple_of(h * vd, vd), vd)
        lb = _lower_bound(lb_ref[:, ks], layer)
        qr = q_ref[:, ks]
        q = (qr * jax.nn.sigmoid(qr)) * q_scale
        forget = lb + (1.0 - lb) * jax.nn.sigmoid(f_ref[:, ks])
        v = i_ref[:, vs]
        b2 = _cumsum_rows(jnp.log2(forget), tri)
        return dict(h=h, vs=vs, q=q, k=1.0 - forget, forget=forget, v=v, v16=v.astype(BF16), b2=b2)

    def carry_state(t):
        q, k, b2, v16, h = t["q"], t["k"], t["b2"], t["v16"], t["h"]
        b_last = b2[c - 1:c, :]
        s_t = s_scr[h]
        t["o"] = lax.dot_general((q * jnp.exp2(b2)).astype(BF16), s_t.astype(BF16), (((1,), (1,)), ((), ())),
                                 preferred_element_type=F32)
        kdec = (k * jnp.exp2(b_last - b2)).astype(BF16)
        s_scr[h] = jnp.exp2(b_last) * s_t + lax.dot_general(v16, kdec, (((0,), (0,)), ((), ())),
                                                            preferred_element_type=F32)
        t["a"] = jnp.zeros((c, c), F32)

    def small_level(t, hf):
        q, k, b2 = t["q"], t["k"], t["b2"]
        if hf == 1:
            qe, ke = q * t["forget"], k
        else:
            e = jnp.exp2(-jnp.abs(b2 - _block_ref_rows(b2, hf)))
            qe, ke = q * e, k * e
        al = lax.dot_general(qe.astype(BF16), ke.astype(BF16), (((1,), (1,)), ((), ())),
                             preferred_element_type=F32)
        t["a"] = jnp.where(pair_bits >= hf, al, t["a"])

    def big_level(t, hf):
        q, k, b2, a_rows = t["q"], t["k"], t["b2"], t["a_rows"]
        q_parts, k_parts, uppers = [], [], []
        for blk in range(0, c, 2 * hf):
            lo, up = slice(blk, blk + hf), slice(blk + hf, blk + 2 * hf)
            r = b2[blk + hf - 1:blk + hf, :]
            q_parts.append(q[up] * jnp.exp2(b2[up] - r))
            k_parts += [k[lo] * jnp.exp2(r - b2[lo]), zero_rows[hf]]
            uppers += [(p, blk) for p in range((blk + hf) // SUBLANES, (blk + 2 * hf) // SUBLANES)]
        al = lax.dot_general(jnp.concatenate(q_parts, axis=0).astype(BF16),
                             jnp.concatenate(k_parts, axis=0).astype(BF16),
                             (((1,), (1,)), ((), ())), preferred_element_type=F32)
        for j, (p, blk) in enumerate(uppers):
            same_block = (lane >= blk) & (lane < blk + hf)
            a_rows[p] = jnp.where(same_block, al[j * SUBLANES:(j + 1) * SUBLANES], a_rows[p])

    def finish(t):
        q, k, v = t["q"], t["k"], t["v"]
        o = t["o"] + jnp.dot(jnp.concatenate(t["a_rows"], axis=0).astype(BF16), t["v16"], preferred_element_type=F32)
        o = o + jnp.sum(q * k, axis=-1, keepdims=True) * v
        o = (o * lax.rsqrt(jnp.mean(o * o, axis=-1, keepdims=True) + RMS_EPS)) * nw
        gr = g_ref[:, t["vs"]]
        og_ref[:, t["vs"]] = (o * (gr * jax.nn.sigmoid(gr))).astype(og_ref.dtype)

    group = min(GLA_HEAD_GROUP, n_heads)

    def head_group(hg, carry):
        ts = [prep(hg * group + j) for j in range(group)]
        for t in ts:
            carry_state(t)
        hf = 1
        while hf < min(SUBLANES, c):
            for t in ts:
                small_level(t, hf)
            hf *= 2
        for t in ts:
            t["a_rows"] = [t["a"][r:r + SUBLANES] for r in range(0, c, SUBLANES)]
        while hf < c:
            for t in ts:
                big_level(t, hf)
            hf *= 2
        for t in ts:
            finish(t)
        return carry

    lax.fori_loop(0, n_heads // group, head_group, 0)

    @pl.when(ci == n_chunks - 1)
    def _():
        def emit(h, carry):
            st_ref[0, h] = s_scr[h].T
            return carry

        lax.fori_loop(0, n_heads, emit, 0)


def gla_prompt(u, lower_bounds, norm_w, layer, batch, seq, n_heads, kd, vd):
    d = n_heads * kd
    c = _tile(seq, GLA_CHUNK)
    nch = seq // c
    n_layers = lower_bounds.shape[0]
    col = lambda j: pl.BlockSpec((c, d), lambda b, i: (b * nch + i, j))
    return pl.pallas_call(
        functools.partial(_gla_prompt_kernel, layer=layer, n_heads=n_heads, kd=kd, vd=vd),
        out_shape=(
            jax.ShapeDtypeStruct((batch * seq, d), BF16),
            jax.ShapeDtypeStruct((batch, n_heads, kd, vd), F32),
        ),
        grid=(batch, nch),
        in_specs=[
            pl.BlockSpec((n_layers, d), lambda b, i: (0, 0)),
            pl.BlockSpec((1, vd), lambda b, i: (0, 0)),
            col(0), col(1), col(2), col(3),
        ],
        out_specs=(
            pl.BlockSpec((c, d), lambda b, i: (b * nch + i, 0)),
            pl.BlockSpec((1, n_heads, kd, vd), lambda b, i: (b, 0, 0, 0)),
        ),
        scratch_shapes=[pltpu.VMEM((n_heads, vd, kd), F32)],
        compiler_params=_params("parallel", "arbitrary"),
        name="gla_prompt",
    )(lower_bounds, norm_w.reshape(1, vd), u, u, u, u)


GLA_SAMPLE_BLOCK = 4


def _gla_sample_kernel(lb_ref, nw_ref, x_ref, s_ref, og_ref, so_ref, *, layer, n_heads, kd):
    nh = n_heads
    lb = _lower_bound(lb_ref[...], layer)[0]

    def one_sequence(b, carry):
        x = x_ref[b]
        qr = x[0:nh]
        q = (qr * jax.nn.sigmoid(qr)) * (kd ** -0.5)
        forget = lb + (1.0 - lb) * jax.nn.sigmoid(x[nh:2 * nh])
        k = 1.0 - forget
        v = x[2 * nh:3 * nh]
        gr = x[3 * nh:4 * nh]
        pad = jnp.zeros((x.shape[0] - 3 * nh, kd), F32)
        cols = jnp.concatenate([q, forget, k, pad], axis=0).T
        outs = []
        for h in range(nh):
            s_new = cols[:, nh + h:nh + h + 1] * s_ref[b, h] + cols[:, 2 * nh + h:2 * nh + h + 1] * v[h:h + 1, :]
            so_ref[b, h] = s_new
            outs.append(jnp.sum(cols[:, h:h + 1] * s_new, axis=0, keepdims=True))
        o = jnp.concatenate(outs, axis=0)
        o = (o * lax.rsqrt(jnp.mean(o * o, axis=-1, keepdims=True) + RMS_EPS)) * nw_ref[...]
        og_ref[b] = o * (gr * jax.nn.sigmoid(gr))
        return carry

    lax.fori_loop(0, x_ref.shape[0], one_sequence, 0)


def gla_sample(u, state, lower_bounds, norm_w, layer, n_heads, kd, vd):
    nb = u.shape[0]
    n_layers = lower_bounds.shape[0]
    rows = 4 * n_heads
    sb = _tile(nb, GLA_SAMPLE_BLOCK)
    return pl.pallas_call(
        functools.partial(_gla_sample_kernel, layer=layer, n_heads=n_heads, kd=kd),
        out_shape=(
            jax.ShapeDtypeStruct((nb, n_heads, vd), F32),
            jax.ShapeDtypeStruct((nb, n_heads, kd, vd), F32),
        ),
        grid=(nb // sb,),
        in_specs=[
            pl.BlockSpec((n_layers, n_heads, kd), lambda i: (0, 0, 0)),
            pl.BlockSpec((1, vd), lambda i: (0, 0)),
            pl.BlockSpec((sb, rows, kd), lambda i: (i, 0, 0)),
            pl.BlockSpec((sb, n_heads, kd, vd), lambda i: (i, 0, 0, 0)),
        ],
        out_specs=(
            pl.BlockSpec((sb, n_heads, vd), lambda i: (i, 0, 0)),
            pl.BlockSpec((sb, n_heads, kd, vd), lambda i: (i, 0, 0, 0)),
        ),
        compiler_params=_params("parallel"),
        name="gla_sample",
    )(lower_bounds.reshape(n_layers, n_heads, kd), norm_w.reshape(1, vd), u.reshape(nb, rows, kd), state)


def kernel(x_prompt, x_sample, cache_k, cache_v, state_hgrn, norm_mix, norm_ffn, norm_final, attn_w_qkv, attn_b_qkv,
           attn_sinks, attn_w_o, attn_b_o, hgrn_w_in, hgrn_lower_bounds, hgrn_norm, hgrn_w_o, ffn_w_in, ffn_w_out):
    batch, seq, d = x_prompt.shape
    nb = x_sample.shape[0]
    assert x_sample.shape[1] == 1
    depth = norm_mix.shape[0]
    window, n_kv, hd = cache_k.shape[2:]
    n_q = attn_sinks.shape[1]
    q_per_kv = n_q // n_kv
    q_dim, kv_dim = n_q * hd, n_kv * hd
    n_heads, kd, vd = state_hgrn.shape[2:]

    hp = x_prompt.reshape(batch * seq, d)
    hs = x_sample.reshape(nb, d)
    wk_p, wv_p, st_p, wk_s, wv_s, st_s = [], [], [], [], [], []
    for i in range(depth):
        if i % 2 == 0:
            a = i // 2
            qkv_p, qkv_s = matmul(hp, hs, attn_w_qkv, a, norm_w=norm_mix[i], bias=attn_b_qkv[a])
            op = attn_prompt(qkv_p, attn_sinks[a], batch, seq, n_kv, q_per_kv, hd, window)
            kv_p = qkv_p.reshape(batch, seq, q_dim + 2 * kv_dim)[:, seq - window:, q_dim:]
            wk_p.append(kv_p[..., :kv_dim].reshape(batch, window, n_kv, hd))
            wv_p.append(kv_p[..., kv_dim:].reshape(batch, window, n_kv, hd))
            os_, wk, wv = attn_sample(
                qkv_s[:, :q_dim].reshape(nb, n_q, hd),
                qkv_s[:, q_dim:q_dim + kv_dim].reshape(nb, 1, kv_dim),
                qkv_s[:, q_dim + kv_dim:].reshape(nb, 1, kv_dim),
                cache_k[a].reshape(nb, window, kv_dim), cache_v[a].reshape(nb, window, kv_dim),
                attn_sinks[a], n_kv, q_per_kv, hd, window)
            wk_s.append(wk.reshape(nb, window, n_kv, hd))
            wv_s.append(wv.reshape(nb, window, n_kv, hd))
            hp, hs = matmul(op, os_.reshape(nb, q_dim).astype(BF16), attn_w_o, a, bias=attn_b_o[a], res=hp, res_s=hs)
        else:
            r = i // 2
            up, us = matmul(hp, hs, hgrn_w_in, r, norm_w=norm_mix[i])
            ogp, sp = gla_prompt(up, hgrn_lower_bounds, hgrn_norm[r], i, batch, seq, n_heads, kd, vd)
            ogs, ss = gla_sample(us, state_hgrn[r], hgrn_lower_bounds, hgrn_norm[r], i, n_heads, kd, vd)
            st_p.append(sp)
            st_s.append(ss)
            hp, hs = matmul(ogp, ogs.reshape(nb, d).astype(BF16), hgrn_w_o, r, res=hp, res_s=hs)
        final_w = norm_final if i == depth - 1 else None
        hp, hs = ffn(hp, hs, norm_ffn[i], ffn_w_in, ffn_w_out, i, final_w=final_w)

    y_prompt = hp.reshape(batch, seq, d)
    y_sample = hs.reshape(nb, 1, d)
    return (y_prompt, y_sample, jnp.stack(wk_p), jnp.stack(wv_p), jnp.stack(st_p), jnp.stack(wk_s), jnp.stack(wv_s),
            jnp.stack(st_s))
```
